```python
import jax, jax.numpy as jnp
from jax import lax
import numpy as np

D_MODEL = 4096
BATCH = 4
SEQ = 4096
DEPTH = 1

N_META = 16
HEAD_DIM = 64
N_Q_HEADS = 32
N_KV_HEADS = 4
GROUP = N_Q_HEADS // N_KV_HEADS
WINDOW = 128
BLOCK = 128
ROPE_THETA = 10000.0
Q_DIM = N_Q_HEADS * HEAD_DIM
KV_DIM = N_KV_HEADS * HEAD_DIM
CONV_DIM = D_MODEL // 2
CONV_WIDTH = 31
FFN_DIM = ((8 * D_MODEL + 3 * 256 - 1) // (3 * 256)) * 256
IN_DIM = Q_DIM + 2 * KV_DIM + 2 * CONV_DIM + 2 * D_MODEL
EPS = 1e-6

kernel_name = "hybrid_swa_sink_conformer_conv_gated_block"


def rms_norm(x, g):
    xf = x.astype(jnp.float32)
    y = xf * lax.rsqrt(jnp.mean(xf * xf, axis=-1, keepdims=True) + EPS)
    return (y * g.astype(jnp.float32)).astype(x.dtype)


def layer_norm(x, g, b):
    xf = x.astype(jnp.float32)
    mu = jnp.mean(xf, axis=-1, keepdims=True)
    xc = xf - mu
    y = xc * lax.rsqrt(jnp.mean(xc * xc, axis=-1, keepdims=True) + EPS)
    return (y * g.astype(jnp.float32) + b.astype(jnp.float32)).astype(x.dtype)


def rope_tables(length):
    pos = jnp.arange(length, dtype=jnp.float32)
    inv_freq = ROPE_THETA ** (-jnp.arange(0, HEAD_DIM, 2, dtype=jnp.float32) / HEAD_DIM)
    ang = pos[:, None] * inv_freq[None, :]
    return jnp.cos(ang), jnp.sin(ang)


def apply_rope(x, cos, sin):
    xf = x.astype(jnp.float32)
    x1, x2 = jnp.split(xf, 2, axis=-1)
    c = cos[None, :, None, :]
    s = sin[None, :, None, :]
    return jnp.concatenate([x1 * c - x2 * s, x2 * c + x1 * s], axis=-1).astype(x.dtype)


def sliding_window_attention(q, k, v, sinks):
    b, length = q.shape[0], q.shape[1]
    pad = BLOCK - N_META
    padded = length + pad
    nb = padded // BLOCK
    padw = ((0, 0), (pad, 0), (0, 0), (0, 0))
    qb = jnp.pad(q, padw).reshape(b, nb, BLOCK, N_KV_HEADS, GROUP, HEAD_DIM)
    kb = jnp.pad(k, padw).reshape(b, nb, BLOCK, N_KV_HEADS, HEAD_DIM)
    vb = jnp.pad(v, padw).reshape(b, nb, BLOCK, N_KV_HEADS, HEAD_DIM)
    k_prev = jnp.concatenate([jnp.zeros_like(kb[:, :1]), kb[:, :-1]], axis=1)
    v_prev = jnp.concatenate([jnp.zeros_like(vb[:, :1]), vb[:, :-1]], axis=1)
    k_band = jnp.concatenate([k_prev, kb], axis=2)
    v_band = jnp.concatenate([v_prev, vb], axis=2)
    k_meta = k[:, :N_META]
    v_meta = v[:, :N_META]
    scale = HEAD_DIM ** -0.5

    s_band = jnp.einsum('bnqhgd,bnkhd->bnhgqk', qb, k_band).astype(jnp.float32) * scale
    s_meta = jnp.einsum('bnqhgd,bmhd->bnhgqm', qb, k_meta).astype(jnp.float32) * scale

    blk = jnp.arange(nb)[:, None]
    q_pos = blk * BLOCK + jnp.arange(BLOCK)[None, :] - pad
    k_pos = (blk - 1) * BLOCK + jnp.arange(2 * BLOCK)[None, :] - pad
    qp = q_pos[:, :, None]
    kp = k_pos[:, None, :]
    band_mask = (kp >= N_META) & (kp <= qp) & (qp - kp < WINDOW)
    meta_mask = jnp.arange(N_META)[None, None, :] <= qp

    neg = jnp.float32(-jnp.inf)
    s_band = jnp.where(band_mask[None, :, None, None], s_band, neg)
    s_meta = jnp.where(meta_mask[None, :, None, None], s_meta, neg)
    sink = jnp.broadcast_to(
        sinks.astype(jnp.float32).reshape(N_KV_HEADS, GROUP)[None, None, :, :, None, None],
        s_band.shape[:-1] + (1,))
    probs = jax.nn.softmax(jnp.concatenate([s_band, s_meta, sink], axis=-1), axis=-1)
    p_band = probs[..., :2 * BLOCK].astype(v.dtype)
    p_meta = probs[..., 2 * BLOCK:2 * BLOCK + N_META].astype(v.dtype)
    o = (jnp.einsum('bnhgqk,bnkhd->bnqhgd', p_band, v_band)
         + jnp.einsum('bnhgqm,bmhd->bnqhgd', p_meta, v_meta))
    return o.reshape(b, padded, Q_DIM)[:, pad:]


def conformer_conv(c_in, conv_w, conv_b, ln_g, ln_b, w_co, b_co):
    a, g = jnp.split(c_in, 2, axis=-1)
    c = a * jax.nn.sigmoid(g)
    c = lax.conv_general_dilated(
        c, conv_w.astype(c.dtype), window_strides=(1,), padding=[(CONV_WIDTH - 1, 0)],
        dimension_numbers=('NWC', 'WIO', 'NWC'), feature_group_count=CONV_DIM) + conv_b
    c = layer_norm(c, ln_g, ln_b)
    c = c * jax.nn.sigmoid(c)
    return c @ w_co + b_co


def mixer_block(u, cos, sin, w_in, b_in, sinks, conv_w, conv_b, ln_g, ln_b,
                w_ao, w_co, b_co, w_out):
    b, length, _ = u.shape
    z = u @ w_in + b_in
    idx = np.cumsum([Q_DIM, KV_DIM, KV_DIM, 2 * CONV_DIM, D_MODEL])
    q, k, v, c_in, gate_a, gate_b = jnp.split(z, idx, axis=-1)
    q = apply_rope(q.reshape(b, length, N_Q_HEADS, HEAD_DIM), cos, sin)
    k = apply_rope(k.reshape(b, length, N_KV_HEADS, HEAD_DIM), cos, sin)
    v = v.reshape(b, length, N_KV_HEADS, HEAD_DIM)
    branch_a = sliding_window_attention(q, k, v, sinks) @ w_ao
    branch_b = conformer_conv(c_in, conv_w, conv_b, ln_g, ln_b, w_co, b_co)
    merged = jax.nn.sigmoid(gate_a) * branch_a + jax.nn.sigmoid(gate_b) * branch_b
    return merged @ w_out


def swiglu(u, w_gate_up, w_down):
    gu = u @ w_gate_up
    g, up = jnp.split(gu, 2, axis=-1)
    return (jax.nn.silu(g) * up) @ w_down


def setup_inputs(seed: int = 0) -> dict:
    key = jax.random.key(seed)
    ks = jax.random.split(key, 20)
    f32 = jnp.float32
    nrm = lambda k, shape, s: jax.random.normal(k, shape, f32) * s
    return {
        "x": nrm(ks[0], (BATCH, SEQ, D_MODEL), 1.0),
        "meta_tokens": nrm(ks[1], (N_META, D_MODEL), 1.0),
        "mix_norm_g": 1.0 + nrm(ks[2], (DEPTH, D_MODEL), 0.02),
        "w_in": nrm(ks[3], (DEPTH, D_MODEL, IN_DIM), D_MODEL ** -0.5),
        "b_in": nrm(ks[4], (DEPTH, IN_DIM), 0.02),
        "attn_sinks": nrm(ks[5], (DEPTH, N_Q_HEADS), 1.0),
        "conv_w": nrm(ks[6], (DEPTH, CONV_WIDTH, 1, CONV_DIM), CONV_WIDTH ** -0.5),
        "conv_b": nrm(ks[7], (DEPTH, CONV_DIM), 0.02),
        "conv_ln_g": 1.0 + nrm(ks[8], (DEPTH, CONV_DIM), 0.02),
        "conv_ln_b": nrm(ks[9], (DEPTH, CONV_DIM), 0.02),
        "w_attn_o": nrm(ks[10], (DEPTH, Q_DIM, D_MODEL), Q_DIM ** -0.5),
        "w_conv_o": nrm(ks[11], (DEPTH, CONV_DIM, D_MODEL), CONV_DIM ** -0.5),
        "b_conv_o": nrm(ks[12], (DEPTH, D_MODEL), 0.02),
        "w_out": nrm(ks[13], (DEPTH, D_MODEL, D_MODEL), D_MODEL ** -0.5),
        "ffn_norm_g": 1.0 + nrm(ks[14], (DEPTH, D_MODEL), 0.02),
        "w_gate_up": nrm(ks[15], (DEPTH, D_MODEL, 2 * FFN_DIM), D_MODEL ** -0.5),
        "w_down": nrm(ks[16], (DEPTH, FFN_DIM, D_MODEL), FFN_DIM ** -0.5),
        "final_norm_g": 1.0 + nrm(ks[17], (D_MODEL,), 0.02),
    }


def reference(x, meta_tokens, mix_norm_g, w_in, b_in, attn_sinks, conv_w, conv_b,
              conv_ln_g, conv_ln_b, w_attn_o, w_conv_o, b_conv_o, w_out,
              ffn_norm_g, w_gate_up, w_down, final_norm_g):
    b = x.shape[0]
    meta = jnp.broadcast_to(meta_tokens[None].astype(x.dtype), (b, N_META, D_MODEL))
    h = jnp.concatenate([meta, x], axis=1)
    cos, sin = rope_tables(h.shape[1])
    for layer in range(DEPTH):
        u = rms_norm(h, mix_norm_g[layer])
        h = h + mixer_block(u, cos, sin, w_in[layer], b_in[layer], attn_sinks[layer],
                            conv_w[layer], conv_b[layer], conv_ln_g[layer], conv_ln_b[layer],
                            w_attn_o[layer], w_conv_o[layer], b_conv_o[layer], w_out[layer])
        h = h + swiglu(rms_norm(h, ffn_norm_g[layer]), w_gate_up[layer], w_down[layer])
    y = rms_norm(h, final_norm_g)
    return y[:, N_META:]
```

```python
import functools

import jax
import jax.numpy as jnp
from jax import lax
from jax.experimental import pallas as pl
from jax.experimental.pallas import tpu as pltpu

N_META = 16
HEAD_DIM = 64
N_Q_HEADS = 32
N_KV_HEADS = 4
GROUP = N_Q_HEADS // N_KV_HEADS
WINDOW = 128
Q_DIM = N_Q_HEADS * HEAD_DIM
KV_DIM = N_KV_HEADS * HEAD_DIM
CONV_WIDTH = 31
ROPE_THETA = 10000.0
EPS = 1e-6

V7X_LANES = 128
V7X_VMEM_BYTES = 64 * 1024 * 1024
VMEM_LIMIT_BYTES = 58 * 1024 * 1024

F32 = jnp.float32
BF16 = jnp.bfloat16


def _params(*sem):
    return pltpu.CompilerParams(dimension_semantics=sem, vmem_limit_bytes=VMEM_LIMIT_BYTES)


def _rmsnorm_kernel(x_ref, g_ref, o_ref):
    x = x_ref[...]
    ms = jnp.mean(x * x, axis=-1, keepdims=True)
    o_ref[...] = (x * lax.rsqrt(ms + EPS) * g_ref[...]).astype(o_ref.dtype)


def _rmsnorm(x, g, out_dtype, tm):
    rows, d = x.shape
    return pl.pallas_call(
        _rmsnorm_kernel,
        out_shape=jax.ShapeDtypeStruct((rows, d), out_dtype),
        grid=(rows // tm,),
        in_specs=[pl.BlockSpec((tm, d), lambda i: (i, 0)),
                  pl.BlockSpec((1, d), lambda i: (0, 0))],
        out_specs=pl.BlockSpec((tm, d), lambda i: (i, 0)),
        compiler_params=_params("arbitrary"),
        name="rmsnorm",
    )(x, g.reshape(1, d))


def _rope(z, cos, sin_signed):
    n = z.shape[-1]
    lane = lax.broadcasted_iota(jnp.int32, z.shape, 1)
    first_half = (lane & (HEAD_DIM - 1)) < HEAD_DIM // 2
    partner = jnp.where(first_half, pltpu.roll(z, n - HEAD_DIM // 2, 1), pltpu.roll(z, HEAD_DIM // 2, 1))
    reps = n // V7X_LANES
    return z * jnp.tile(cos, (1, reps)) + partner * jnp.tile(sin_signed, (1, reps))


def _qkv_kernel(u_ref, w_ref, b_ref, cos_ref, sin_ref, o_ref, *, tn, j0):
    j = pl.program_id(1) + j0
    z = jnp.dot(u_ref[...], w_ref[...].astype(BF16), preferred_element_type=F32) + b_ref[...]
    roped = _rope(z, cos_ref[...], sin_ref[...])
    lane = lax.broadcasted_iota(jnp.int32, z.shape, 1)
    is_q = j < Q_DIM // tn
    use_rope = jnp.logical_or(is_q, lane < KV_DIM)
    scale = jnp.where(is_q, HEAD_DIM ** -0.5, 1.0).astype(F32)
    o_ref[...] = (jnp.where(use_rope, roped, z) * scale).astype(o_ref.dtype)


def _qkv_proj(u, w_in, b_in, cos, sin_signed, *, tm, tn, j0, nj, rows_per_seq):
    rows, d = u.shape
    seq_tiles = rows_per_seq // tm
    return pl.pallas_call(
        functools.partial(_qkv_kernel, tn=tn, j0=j0),
        out_shape=jax.ShapeDtypeStruct((rows, nj * tn), BF16),
        grid=(rows // tm, nj),
        in_specs=[pl.BlockSpec((tm, d), lambda i, j: (i, 0)),
                  pl.BlockSpec((d, tn), lambda i, j: (0, j + j0)),
                  pl.BlockSpec((1, tn), lambda i, j: (0, j + j0)),
                  pl.BlockSpec((tm, V7X_LANES), lambda i, j: (i % seq_tiles, 0)),
                  pl.BlockSpec((tm, V7X_LANES), lambda i, j: (i % seq_tiles, 0))],
        out_specs=pl.BlockSpec((tm, tn), lambda i, j: (i, j)),
        compiler_params=_params("arbitrary", "arbitrary"),
        name="qkv_proj",
    )(u, w_in, b_in, cos, sin_signed)


def _pair_kernel(u_ref, wa_ref, wb_ref, *rest, combine):
    a = jnp.dot(u_ref[...], wa_ref[...].astype(BF16), preferred_element_type=F32)
    b = jnp.dot(u_ref[...], wb_ref[...].astype(BF16), preferred_element_type=F32)
    if len(rest) == 3:
        ba_ref, bb_ref, o_ref = rest
        a, b = a + ba_ref[...], b + bb_ref[...]
    else:
        (o_ref,) = rest
    o_ref[...] = combine(a, b).astype(o_ref.dtype)


def _glu(a, g):
    return a * jax.nn.sigmoid(g)


def _swiglu(g, up):
    return g * jax.nn.sigmoid(g) * up


def _pair_proj(u, w, b, *, col_a, col_b, width, tm, tn, combine, out_dtype, name):
    rows, d = u.shape
    ja, jb = col_a // tn, col_b // tn
    assert ja * tn == col_a and jb * tn == col_b and width % tn == 0
    in_specs = [pl.BlockSpec((tm, d), lambda i, j: (i, 0)),
                pl.BlockSpec((d, tn), lambda i, j: (0, j + ja)),
                pl.BlockSpec((d, tn), lambda i, j: (0, j + jb))]
    args = [u, w, w]
    if b is not None:
        in_specs += [pl.BlockSpec((1, tn), lambda i, j: (0, j + ja)),
                     pl.BlockSpec((1, tn), lambda i, j: (0, j + jb))]
        args += [b, b]
    return pl.pallas_call(
        functools.partial(_pair_kernel, combine=combine),
        out_shape=jax.ShapeDtypeStruct((rows, width), out_dtype),
        grid=(rows // tm, width // tn),
        in_specs=in_specs,
        out_specs=pl.BlockSpec((tm, tn), lambda i, j: (i, j)),
        compiler_params=_params("arbitrary", "arbitrary"),
        name=name,
    )(*args)


def _gates_kernel(u_ref, w_ref, b_ref, o_ref):
    z = jnp.dot(u_ref[...], w_ref[...].astype(BF16), preferred_element_type=F32) + b_ref[...]
    o_ref[...] = jax.nn.sigmoid(z).astype(o_ref.dtype)


def _gates_proj(u, w, b, *, col0, width, tm, tn):
    rows, d = u.shape
    j0 = col0 // tn
    assert j0 * tn == col0 and width % tn == 0
    return pl.pallas_call(
        _gates_kernel,
        out_shape=jax.ShapeDtypeStruct((rows, width), BF16),
        grid=(rows // tm, width // tn),
        in_specs=[pl.BlockSpec((tm, d), lambda i, j: (i, 0)),
                  pl.BlockSpec((d, tn), lambda i, j: (0, j + j0)),
                  pl.BlockSpec((1, tn), lambda i, j: (0, j + j0))],
        out_specs=pl.BlockSpec((tm, tn), lambda i, j: (i, j)),
        compiler_params=_params("arbitrary", "arbitrary"),
        name="gates_proj",
    )(u, w, b)


ATT_KEYS = 3 * WINDOW
PAIRS = GROUP // 2


def _attn_kernel(sink_ref, q_ref, kc_ref, kp_ref, vc_ref, vp_ref, kvm_ref, o_ref):
    first_block = pl.program_id(1) == 0
    rows = PAIRS * WINDOW
    r = lax.broadcasted_iota(jnp.int32, (rows, ATT_KEYS), 0) & (WINDOW - 1)
    c = lax.broadcasted_iota(jnp.int32, (rows, ATT_KEYS), 1)
    c_min = jnp.where(first_block, WINDOW, 0)
    band = (c > r) & (c <= r + WINDOW) & (c >= c_min)
    meta = (c >= 2 * WINDOW) & (c < 2 * WINDOW + N_META)
    bias = jnp.where(band | meta, 0.0, -jnp.inf).astype(F32)

    row_pair = lax.shift_right_logical(lax.broadcasted_iota(jnp.int32, (rows, 1), 0), WINDOW.bit_length() - 1)
    lane = lax.broadcasted_iota(jnp.int32, (ATT_KEYS, V7X_LANES), 1)
    low = lane < HEAD_DIM
    zero_pad = jnp.zeros((WINDOW - N_META, V7X_LANES), F32)

    for g in range(N_KV_HEADS):
        tile, upper = g // 2, g % 2
        sl = slice(tile * V7X_LANES, (tile + 1) * V7X_LANES)

        def halves(prev_ref, cur_ref, meta_cols):
            x = jnp.concatenate([prev_ref[:, sl].astype(F32), cur_ref[:, sl].astype(F32),
                                 kvm_ref[:, meta_cols].astype(F32), zero_pad], axis=0)
            swapped = pltpu.roll(x, HEAD_DIM, 1)
            lo_src, hi_src = (swapped, x) if upper else (x, swapped)
            return (jnp.where(low, lo_src, 0.0).astype(BF16), jnp.where(low, 0.0, hi_src).astype(BF16))

        k_even, k_odd = halves(kp_ref, kc_ref, slice(tile * V7X_LANES, (tile + 1) * V7X_LANES))
        v_even, v_odd = halves(vp_ref, vc_ref, slice(KV_DIM + tile * V7X_LANES, KV_DIM + (tile + 1) * V7X_LANES))

        qp = jnp.concatenate(
            [q_ref[:, g * GROUP * HEAD_DIM + m * V7X_LANES: g * GROUP * HEAD_DIM + (m + 1) * V7X_LANES]
             for m in range(PAIRS)], axis=0)

        out = None
        for parity, (kz, vz) in enumerate(((k_even, v_even), (k_odd, v_odd))):
            sink = jnp.zeros((rows, 1), F32)
            for m in range(PAIRS):
                sink = jnp.where(row_pair == m, sink_ref[g * GROUP + 2 * m + parity], sink)
            s = lax.dot_general(qp, kz, (((1,), (1,)), ((), ())), preferred_element_type=F32) + bias
            mx = jnp.maximum(jnp.max(s, axis=1, keepdims=True), sink)
            p = jnp.exp(s - mx)
            denom = jnp.sum(p, axis=1, keepdims=True) + jnp.exp(sink - mx)
            o = jnp.dot(p.astype(BF16), vz, preferred_element_type=F32) * (1.0 / denom)
            out = o if out is None else out + o
        for m in range(PAIRS):
            col = g * GROUP * HEAD_DIM + m * V7X_LANES
            o_ref[:, col:col + V7X_LANES] = out[m * WINDOW:(m + 1) * WINDOW, :].astype(o_ref.dtype)


def _attention(qkv, kv_meta, sinks, *, batch, seq):
    nb = seq // WINDOW
    kcol, vcol = Q_DIM // KV_DIM, Q_DIM // KV_DIM + 1

    def cur(col):
        return lambda b, j: (b * nb + j, col)

    def prev(col):
        return lambda b, j: (jnp.maximum(b * nb + j - 1, 0), col)

    return pl.pallas_call(
        _attn_kernel,
        out_shape=jax.ShapeDtypeStruct((batch * seq, Q_DIM), BF16),
        grid=(batch, nb),
        in_specs=[pl.BlockSpec(memory_space=pltpu.SMEM),
                  pl.BlockSpec((WINDOW, Q_DIM), lambda b, j: (b * nb + j, 0)),
                  pl.BlockSpec((WINDOW, KV_DIM), cur(kcol)),
                  pl.BlockSpec((WINDOW, KV_DIM), prev(kcol)),
                  pl.BlockSpec((WINDOW, KV_DIM), cur(vcol)),
                  pl.BlockSpec((WINDOW, KV_DIM), prev(vcol)),
                  pl.BlockSpec((N_META, 2 * KV_DIM), lambda b, j: (0, 0))],
        out_specs=pl.BlockSpec((WINDOW, Q_DIM), lambda b, j: (b * nb + j, 0)),
        compiler_params=_params("arbitrary", "arbitrary"),
        name="swa_attention",
    )(sinks, qkv, qkv, qkv, qkv, qkv, kv_meta)


CONV_HALO = 32
CONV_ROWS = 16
CONV_LANES = 512


def _conv_kernel(c_ref, halo_ref, cm_ref, w_ref, cb_ref, g_ref, b_ref, o_ref, ext_ref, acc_ref, *, tt):
    t = pl.program_id(1)
    channels = c_ref.shape[1]

    @pl.when(t == 0)
    def _():
        ext_ref[0:CONV_HALO - N_META, :] = jnp.zeros((CONV_HALO - N_META, channels), F32)
        ext_ref[CONV_HALO - N_META:CONV_HALO, :] = cm_ref[...]

    @pl.when(t > 0)
    def _():
        ext_ref[0:CONV_HALO, :] = halo_ref[...]

    ext_ref[CONV_HALO:, :] = c_ref[...]
    lead = CONV_HALO - (CONV_WIDTH - 1)

    def row_step(rc, carry):
        r0 = pl.multiple_of(rc * CONV_ROWS, CONV_ROWS)
        for lc in range(channels // CONV_LANES):
            ls = slice(lc * CONV_LANES, (lc + 1) * CONV_LANES)
            win = ext_ref[pl.ds(r0, CONV_ROWS + CONV_HALO), ls]
            acc = jnp.broadcast_to(cb_ref[:, ls], (CONV_ROWS, CONV_LANES))
            for k in range(CONV_WIDTH):
                acc = acc + win[lead + k:lead + k + CONV_ROWS, :] * w_ref[k:k + 1, ls]
            acc_ref[pl.ds(r0, CONV_ROWS), ls] = acc
        y = acc_ref[pl.ds(r0, CONV_ROWS), :]
        mu = jnp.mean(y, axis=-1, keepdims=True)
        yc = y - mu
        var = jnp.mean(yc * yc, axis=-1, keepdims=True)
        yn = yc * lax.rsqrt(var + EPS) * g_ref[...] + b_ref[...]
        o_ref[pl.ds(r0, CONV_ROWS), :] = (yn * jax.nn.sigmoid(yn)).astype(o_ref.dtype)
        return carry

    lax.fori_loop(0, tt // CONV_ROWS, row_step, 0)


def _conv_module(c, c_meta, conv_w, conv_b, ln_g, ln_b, *, batch, seq, tt):
    rows, channels = c.shape
    nt = seq // tt
    halo_per_tile = tt // CONV_HALO
    return pl.pallas_call(
        functools.partial(_conv_kernel, tt=tt),
        out_shape=jax.ShapeDtypeStruct((rows, channels), BF16),
        grid=(batch, nt),
        in_specs=[pl.BlockSpec((tt, channels), lambda b, t: (b * nt + t, 0)),
                  pl.BlockSpec((CONV_HALO, channels),
                               lambda b, t: (jnp.maximum((b * nt + t) * halo_per_tile - 1, 0), 0)),
                  pl.BlockSpec((N_META, channels), lambda b, t: (0, 0)),
                  pl.BlockSpec((CONV_WIDTH, channels), lambda b, t: (0, 0)),
                  pl.BlockSpec((1, channels), lambda b, t: (0, 0)),
                  pl.BlockSpec((1, channels), lambda b, t: (0, 0)),
                  pl.BlockSpec((1, channels), lambda b, t: (0, 0))],
        out_specs=pl.BlockSpec((tt, channels), lambda b, t: (b * nt + t, 0)),
        scratch_shapes=[pltpu.VMEM((tt + CONV_HALO, channels), F32),
                        pltpu.VMEM((tt, channels), F32)],
        compiler_params=_params("arbitrary", "arbitrary"),
        name="conformer_conv",
    )(c, c, c_meta, conv_w, conv_b.reshape(1, channels), ln_g.reshape(1, channels), ln_b.reshape(1, channels))


def _merge_kernel(a_ref, c_ref, wa_ref, wc_ref, bc_ref, ga_ref, gb_ref, o_ref):
    a = jnp.dot(a_ref[...], wa_ref[...].astype(BF16), preferred_element_type=F32)
    b = jnp.dot(c_ref[...], wc_ref[...].astype(BF16), preferred_element_type=F32) + bc_ref[...]
    o_ref[...] = (ga_ref[...].astype(F32) * a + gb_ref[...].astype(F32) * b).astype(o_ref.dtype)


def _merge(attn, conv, w_ao, w_co, b_co, gates, *, tm, tn):
    rows, k = attn.shape
    d = w_ao.shape[1]
    nj = d // tn
    return pl.pallas_call(
        _merge_kernel,
        out_shape=jax.ShapeDtypeStruct((rows, d), BF16),
        grid=(rows // tm, nj),
        in_specs=[pl.BlockSpec((tm, k), lambda i, j: (i, 0)),
                  pl.BlockSpec((tm, k), lambda i, j: (i, 0)),
                  pl.BlockSpec((k, tn), lambda i, j: (0, j)),
                  pl.BlockSpec((k, tn), lambda i, j: (0, j)),
                  pl.BlockSpec((1, tn), lambda i, j: (0, j)),
                  pl.BlockSpec((tm, tn), lambda i, j: (i, j)),
                  pl.BlockSpec((tm, tn), lambda i, j: (i, j + nj))],
        out_specs=pl.BlockSpec((tm, tn), lambda i, j: (i, j)),
        compiler_params=_params("arbitrary", "arbitrary"),
        name="gated_merge",
    )(attn, conv, w_ao, w_co, b_co.reshape(1, d), gates, gates)


def _residual_kernel(x_ref, w_ref, r_ref, o_ref):
    o_ref[...] = r_ref[...] + jnp.dot(x_ref[...], w_ref[...].astype(BF16), preferred_element_type=F32)


def _residual_matmul(x, w, resid, *, tm, tn, single_buffer_x=False):
    rows, k = x.shape
    n = w.shape[1]
    x_spec = (pl.BlockSpec((tm, k), lambda i, j: (i, 0), pipeline_mode=pl.Buffered(1))
              if single_buffer_x else pl.BlockSpec((tm, k), lambda i, j: (i, 0)))
    return pl.pallas_call(
        _residual_kernel,
        out_shape=jax.ShapeDtypeStruct((rows, n), F32),
        grid=(rows // tm, n // tn),
        in_specs=[x_spec,
                  pl.BlockSpec((k, tn), lambda i, j: (0, j)),
                  pl.BlockSpec((tm, tn), lambda i, j: (i, j))],
        out_specs=pl.BlockSpec((tm, tn), lambda i, j: (i, j)),
        compiler_params=_params("arbitrary", "arbitrary"),
        name="residual_matmul",
    )(x, w, resid)


def _rope_tables(first_pos, length):
    pos = jnp.arange(first_pos, first_pos + length, dtype=F32)
    inv_freq = ROPE_THETA ** (-jnp.arange(0, HEAD_DIM, 2, dtype=F32) / HEAD_DIM)
    ang = pos[:, None] * inv_freq[None, :]
    cos, sin = jnp.cos(ang), jnp.sin(ang)
    reps = V7X_LANES // HEAD_DIM
    return (jnp.tile(jnp.concatenate([cos, cos], axis=1), (1, reps)),
            jnp.tile(jnp.concatenate([-sin, sin], axis=1), (1, reps)))


def kernel(x, meta_tokens, mix_norm_g, w_in, b_in, attn_sinks, conv_w, conv_b, conv_ln_g, conv_ln_b,
           w_attn_o, w_conv_o, b_conv_o, w_out, ffn_norm_g, w_gate_up, w_down, final_norm_g):
    batch, seq, d = x.shape
    depth = w_in.shape[0]
    assert depth == 1, "one layer: the meta rows' outputs are never needed"
    conv_dim = conv_w.shape[-1]
    ffn = w_down.shape[1]
    in_dim = w_in.shape[-1]
    rows = batch * seq
    col_conv = Q_DIM + 2 * KV_DIM
    col_gate = col_conv + 2 * conv_dim
    assert in_dim == col_gate + 2 * d

    w_in2 = w_in.reshape(d, in_dim)
    b_in2 = b_in.reshape(1, in_dim)
    h0 = x.reshape(rows, d)
    tm = 1024
    qkv_tn = 512

    u = _rmsnorm(h0, mix_norm_g[0], BF16, tm=256)
    u_meta = _rmsnorm(meta_tokens, mix_norm_g[0], BF16, tm=N_META)

    cos, sin_signed = _rope_tables(N_META, seq)
    cos_m, sin_m = _rope_tables(0, N_META)
    qkv = _qkv_proj(u, w_in2, b_in2, cos, sin_signed, tm=tm, tn=qkv_tn, j0=0,
                    nj=(Q_DIM + 2 * KV_DIM) // qkv_tn, rows_per_seq=seq)
    kv_meta = _qkv_proj(u_meta, w_in2, b_in2, cos_m, sin_m, tm=N_META, tn=qkv_tn, j0=Q_DIM // qkv_tn,
                        nj=1, rows_per_seq=N_META)
    glu = functools.partial(_pair_proj, w=w_in2, b=b_in2, col_a=col_conv, col_b=col_conv + conv_dim,
                            width=conv_dim, tn=256, combine=_glu, out_dtype=F32, name="conv_glu_proj")
    c = glu(u, tm=tm)
    c_meta = glu(u_meta, tm=N_META)
    gates = _gates_proj(u, w_in2, b_in2, col0=col_gate, width=2 * d, tm=tm, tn=512)

    attn = _attention(qkv, kv_meta, attn_sinks[0], batch=batch, seq=seq)
    conv = _conv_module(c, c_meta, conv_w.reshape(CONV_WIDTH, conv_dim), conv_b[0], conv_ln_g[0],
                        conv_ln_b[0], batch=batch, seq=seq, tt=256)

    merged = _merge(attn, conv, w_attn_o[0], w_conv_o[0], b_conv_o[0], gates, tm=tm, tn=512)
    h1 = _residual_matmul(merged, w_out[0], h0, tm=tm, tn=512)

    u2 = _rmsnorm(h1, ffn_norm_g[0], BF16, tm=256)
    act = _pair_proj(u2, w=w_gate_up[0], b=None, col_a=0, col_b=ffn, width=ffn,
                     tm=tm, tn=256, combine=_swiglu, out_dtype=BF16, name="ffn_gate_up")
    h2 = _residual_matmul(act, w_down[0], h1, tm=tm, tn=256, single_buffer_x=True)

    y = _rmsnorm(h2, final_norm_g, x.dtype, tm=256)
    return y.reshape(batch, seq, d)
```

```python
import functools

import jax
import jax.numpy as jnp
from jax import lax
from jax.experimental import pallas as pl
from jax.experimental.pallas import tpu as pltpu

N_META = 16
HEAD_DIM = 64
N_Q_HEADS = 32
N_KV_HEADS = 4
GROUP = N_Q_HEADS // N_KV_HEADS
WINDOW = 128
Q_DIM = N_Q_HEADS * HEAD_DIM
KV_DIM = N_KV_HEADS * HEAD_DIM
CONV_WIDTH = 31
ROPE_THETA = 10000.0
EPS = 1e-6

V7X_LANES = 128
V7X_VMEM_BYTES = 64 * 1024 * 1024
VMEM_LIMIT_BYTES = 58 * 1024 * 1024

F32 = jnp.float32
BF16 = jnp.bfloat16


def _params(*sem):
    return pltpu.CompilerParams(dimension_semantics=sem, vmem_limit_bytes=VMEM_LIMIT_BYTES)


def _rmsnorm_kernel(x_ref, g_ref, o_ref):
    x = x_ref[...]
    ms = jnp.mean(x * x, axis=-1, keepdims=True)
    o_ref[...] = (x * lax.rsqrt(ms + EPS) * g_ref[...]).astype(o_ref.dtype)


def _rmsnorm(x, g, out_dtype, tm):
    rows, d = x.shape
    return pl.pallas_call(
        _rmsnorm_kernel,
        out_shape=jax.ShapeDtypeStruct((rows, d), out_dtype),
        grid=(rows // tm,),
        in_specs=[pl.BlockSpec((tm, d), lambda i: (i, 0)),
                  pl.BlockSpec((1, d), lambda i: (0, 0))],
        out_specs=pl.BlockSpec((tm, d), lambda i: (i, 0)),
        compiler_params=_params("arbitrary"),
        name="rmsnorm",
    )(x, g.reshape(1, d))


def _rope(z, cos, sin_signed):
    n = z.shape[-1]
    lane = lax.broadcasted_iota(jnp.int32, z.shape, 1)
    first_half = (lane & (HEAD_DIM - 1)) < HEAD_DIM // 2
    partner = jnp.where(first_half, pltpu.roll(z, n - HEAD_DIM // 2, 1), pltpu.roll(z, HEAD_DIM // 2, 1))
    reps = n // V7X_LANES
    return z * jnp.tile(cos, (1, reps)) + partner * jnp.tile(sin_signed, (1, reps))


def _qkv_kernel(u_ref, w_ref, b_ref, cos_ref, sin_ref, o_ref, *, tn, j0):
    j = pl.program_id(1) + j0
    z = jnp.dot(u_ref[...], w_ref[...].astype(BF16), preferred_element_type=F32) + b_ref[...]
    roped = _rope(z, cos_ref[...], sin_ref[...])
    lane = lax.broadcasted_iota(jnp.int32, z.shape, 1)
    is_q = j < Q_DIM // tn
    use_rope = jnp.logical_or(is_q, lane < KV_DIM)
    scale = jnp.where(is_q, HEAD_DIM ** -0.5, 1.0).astype(F32)
    o_ref[...] = (jnp.where(use_rope, roped, z) * scale).astype(o_ref.dtype)


def _qkv_proj(u, w_in, b_in, cos, sin_signed, *, tm, tn, j0, nj, rows_per_seq):
    rows, d = u.shape
    seq_tiles = rows_per_seq // tm
    return pl.pallas_call(
        functools.partial(_qkv_kernel, tn=tn, j0=j0),
        out_shape=jax.ShapeDtypeStruct((rows, nj * tn), BF16),
        grid=(rows // tm, nj),
        in_specs=[pl.BlockSpec((tm, d), lambda i, j: (i, 0)),
                  pl.BlockSpec((d, tn), lambda i, j: (0, j + j0)),
                  pl.BlockSpec((1, tn), lambda i, j: (0, j + j0)),
                  pl.BlockSpec((tm, V7X_LANES), lambda i, j: (i % seq_tiles, 0)),
                  pl.BlockSpec((tm, V7X_LANES), lambda i, j: (i % seq_tiles, 0))],
        out_specs=pl.BlockSpec((tm, tn), lambda i, j: (i, j)),
        compiler_params=_params("arbitrary", "arbitrary"),
        name="qkv_proj",
    )(u, w_in, b_in, cos, sin_signed)


def _pair_kernel(u_ref, wa_ref, wb_ref, *rest, combine):
    a = jnp.dot(u_ref[...], wa_ref[...].astype(BF16), preferred_element_type=F32)
    b = jnp.dot(u_ref[...], wb_ref[...].astype(BF16), preferred_element_type=F32)
    if len(rest) == 3:
        ba_ref, bb_ref, o_ref = rest
        a, b = a + ba_ref[...], b + bb_ref[...]
    else:
        (o_ref,) = rest
    o_ref[...] = combine(a, b).astype(o_ref.dtype)


def _glu(a, g):
    return a * jax.nn.sigmoid(g)


def _swiglu(g, up):
    return g * jax.nn.sigmoid(g) * up


def _pair_proj(u, w, b, *, col_a, col_b, width, tm, tn, combine, out_dtype, name):
    rows, d = u.shape
    ja, jb = col_a // tn, col_b // tn
    assert ja * tn == col_a and jb * tn == col_b and width % tn == 0
    in_specs = [pl.BlockSpec((tm, d), lambda i, j: (i, 0)),
                pl.BlockSpec((d, tn), lambda i, j: (0, j + ja)),
                pl.BlockSpec((d, tn), lambda i, j: (0, j + jb))]
    args = [u, w, w]
    if b is not None:
        in_specs += [pl.BlockSpec((1, tn), lambda i, j: (0, j + ja)),
                     pl.BlockSpec((1, tn), lambda i, j: (0, j + jb))]
        args += [b, b]
    return pl.pallas_call(
        functools.partial(_pair_kernel, combine=combine),
        out_shape=jax.ShapeDtypeStruct((rows, width), out_dtype),
        grid=(rows // tm, width // tn),
        in_specs=in_specs,
        out_specs=pl.BlockSpec((tm, tn), lambda i, j: (i, j)),
        compiler_params=_params("arbitrary", "arbitrary"),
        name=name,
    )(*args)


def _gates_kernel(u_ref, w_ref, b_ref, o_ref):
    z = jnp.dot(u_ref[...], w_ref[...].astype(BF16), preferred_element_type=F32) + b_ref[...]
    o_ref[...] = jax.nn.sigmoid(z).astype(o_ref.dtype)


def _gates_proj(u, w, b, *, col0, width, tm, tn):
    rows, d = u.shape
    j0 = col0 // tn
    assert j0 * tn == col0 and width % tn == 0
    return pl.pallas_call(
        _gates_kernel,
        out_shape=jax.ShapeDtypeStruct((rows, width), BF16),
        grid=(rows // tm, width // tn),
        in_specs=[pl.BlockSpec((tm, d), lambda i, j: (i, 0)),
                  pl.BlockSpec((d, tn), lambda i, j: (0, j + j0)),
                  pl.BlockSpec((1, tn), lambda i, j: (0, j + j0))],
        out_specs=pl.BlockSpec((tm, tn), lambda i, j: (i, j)),
        compiler_params=_params("arbitrary", "arbitrary"),
        name="gates_proj",
    )(u, w, b)


ATT_KEYS = 3 * WINDOW
PAIRS = GROUP // 2


def _attn_kernel(sink_ref, q_ref, kc_ref, kp_ref, vc_ref, vp_ref, kvm_ref, o_ref):
    first_block = pl.program_id(1) == 0
    rows = PAIRS * WINDOW
    r = lax.broadcasted_iota(jnp.int32, (rows, ATT_KEYS), 0) & (WINDOW - 1)
    c = lax.broadcasted_iota(jnp.int32, (rows, ATT_KEYS), 1)
    c_min = jnp.where(first_block, WINDOW, 0)
    band = (c > r) & (c <= r + WINDOW) & (c >= c_min)
    meta = (c >= 2 * WINDOW) & (c < 2 * WINDOW + N_META)
    bias = jnp.where(band | meta, 0.0, -jnp.inf).astype(F32)

    row_pair = lax.shift_right_logical(lax.broadcasted_iota(jnp.int32, (rows, 1), 0), WINDOW.bit_length() - 1)
    lane = lax.broadcasted_iota(jnp.int32, (ATT_KEYS, V7X_LANES), 1)
    low = lane < HEAD_DIM
    zero_pad = jnp.zeros((WINDOW - N_META, V7X_LANES), F32)

    for g in range(N_KV_HEADS):
        tile, upper = g // 2, g % 2
        sl = slice(tile * V7X_LANES, (tile + 1) * V7X_LANES)

        def halves(prev_ref, cur_ref, meta_cols):
            x = jnp.concatenate([prev_ref[:, sl].astype(F32), cur_ref[:, sl].astype(F32),
                                 kvm_ref[:, meta_cols].astype(F32), zero_pad], axis=0)
            swapped = pltpu.roll(x, HEAD_DIM, 1)
            lo_src, hi_src = (swapped, x) if upper else (x, swapped)
            return (jnp.where(low, lo_src, 0.0).astype(BF16), jnp.where(low, 0.0, hi_src).astype(BF16))

        k_even, k_odd = halves(kp_ref, kc_ref, slice(tile * V7X_LANES, (tile + 1) * V7X_LANES))
        v_even, v_odd = halves(vp_ref, vc_ref, slice(KV_DIM + tile * V7X_LANES, KV_DIM + (tile + 1) * V7X_LANES))

        qp = jnp.concatenate(
            [q_ref[:, g * GROUP * HEAD_DIM + m * V7X_LANES: g * GROUP * HEAD_DIM + (m + 1) * V7X_LANES]
             for m in range(PAIRS)], axis=0)

        out = None
        for parity, (kz, vz) in enumerate(((k_even, v_even), (k_odd, v_odd))):
            sink = jnp.zeros((rows, 1), F32)
            for m in range(PAIRS):
                sink = jnp.where(row_pair == m, sink_ref[g * GROUP + 2 * m + parity], sink)
            s = lax.dot_general(qp, kz, (((1,), (1,)), ((), ())), preferred_element_type=F32) + bias
            mx = jnp.maximum(jnp.max(s, axis=1, keepdims=True), sink)
            p = jnp.exp(s - mx)
            denom = jnp.sum(p, axis=1, keepdims=True) + jnp.exp(sink - mx)
            o = jnp.dot(p.astype(BF16), vz, preferred_element_type=F32) * (1.0 / denom)
            out = o if out is None else out + o
        for m in range(PAIRS):
            col = g * GROUP * HEAD_DIM + m * V7X_LANES
            o_ref[:, col:col + V7X_LANES] = out[m * WINDOW:(m + 1) * WINDOW, :].astype(o_ref.dtype)


def _attention(qkv, kv_meta, sinks, *, batch, seq):
    nb = seq // WINDOW
    kcol, vcol = Q_DIM // KV_DIM, Q_DIM // KV_DIM + 1

    def cur(col):
        return lambda b, j: (b * nb + j, col)

    def prev(col):
        return lambda b, j: (jnp.maximum(b * nb + j - 1, 0), col)

    return pl.pallas_call(
        _attn_kernel,
        out_shape=jax.ShapeDtypeStruct((batch * seq, Q_DIM), BF16),
        grid=(batch, nb),
        in_specs=[pl.BlockSpec(memory_space=pltpu.SMEM),
                  pl.BlockSpec((WINDOW, Q_DIM), lambda b, j: (b * nb + j, 0)),
                  pl.BlockSpec((WINDOW, KV_DIM), cur(kcol)),
                  pl.BlockSpec((WINDOW, KV_DIM), prev(kcol)),
                  pl.BlockSpec((WINDOW, KV_DIM), cur(vcol)),
                  pl.BlockSpec((WINDOW, KV_DIM), prev(vcol)),
                  pl.BlockSpec((N_META, 2 * KV_DIM), lambda b, j: (0, 0))],
        out_specs=pl.BlockSpec((WINDOW, Q_DIM), lambda b, j: (b * nb + j, 0)),
        compiler_params=_params("arbitrary", "arbitrary"),
        name="swa_attention",
    )(sinks, qkv, qkv, qkv, qkv, qkv, kv_meta)


CONV_HALO = 32
CONV_ROWS = 8
CONV_CHAINS = 8
LN_ROWS = 16


def _conv_kernel(c_ref, halo_ref, cm_ref, w_ref, cb_ref, g_ref, b_ref, o_ref, ext_ref, y_ref, s1_ref, s2_ref,
                 *, tt):
    t = pl.program_id(1)
    lane_tiles = ext_ref.shape[0]
    lead = CONV_HALO - (CONV_WIDTH - 1)

    for ct in range(lane_tiles):
        ls = slice(ct * V7X_LANES, (ct + 1) * V7X_LANES)
        ext_ref[ct, CONV_HALO:, :] = c_ref[:, ls]

        @pl.when(t == 0)
        def _():
            ext_ref[ct, 0:CONV_HALO - N_META, :] = jnp.zeros((CONV_HALO - N_META, V7X_LANES), F32)
            ext_ref[ct, CONV_HALO - N_META:CONV_HALO, :] = cm_ref[:, ls]

        @pl.when(t > 0)
        def _():
            ext_ref[ct, 0:CONV_HALO, :] = halo_ref[:, ls]

    for ct in range(lane_tiles):
        ls = slice(ct * V7X_LANES, (ct + 1) * V7X_LANES)
        taps = [jnp.broadcast_to(w_ref[k:k + 1, ls], (CONV_ROWS, V7X_LANES)) for k in range(CONV_WIDTH)]
        bias = jnp.broadcast_to(cb_ref[:, ls], (CONV_ROWS, V7X_LANES))

        def tap_step(rc, carry):
            r0 = pl.multiple_of(rc * (CONV_ROWS * CONV_CHAINS), CONV_ROWS * CONV_CHAINS)
            accs = [bias] * CONV_CHAINS
            for k in range(CONV_WIDTH):
                for j in range(CONV_CHAINS):
                    x = ext_ref[ct, pl.ds(r0 + j * CONV_ROWS + lead + k, CONV_ROWS), :]
                    accs[j] = accs[j] + x * taps[k]
            for j in range(CONV_CHAINS):
                rows = pl.ds(r0 + j * CONV_ROWS, CONV_ROWS)
                y_ref[rows, ls] = accs[j]
                s1_ref[rows, :] = accs[j] if ct == 0 else s1_ref[rows, :] + accs[j]
            return carry

        lax.fori_loop(0, tt // (CONV_ROWS * CONV_CHAINS), tap_step, 0)

    inv_n = 1.0 / (lane_tiles * V7X_LANES)
    mu = jnp.sum(s1_ref[...], axis=1, keepdims=True) * inv_n
    s1_ref[...] = jnp.broadcast_to(mu, s1_ref.shape)

    def var_step(rc, carry):
        r0 = pl.multiple_of(rc * CONV_ROWS, CONV_ROWS)
        rows = pl.ds(r0, CONV_ROWS)
        mu_b = s1_ref[rows, :]
        parts = [None, None]
        for ct in range(lane_tiles):
            d = y_ref[rows, ct * V7X_LANES:(ct + 1) * V7X_LANES] - mu_b
            parts[ct % 2] = d * d if parts[ct % 2] is None else parts[ct % 2] + d * d
        s2_ref[rows, :] = parts[0] + parts[1]
        return carry

    lax.fori_loop(0, tt // CONV_ROWS, var_step, 0, unroll=4)
    var = jnp.sum(s2_ref[...], axis=1, keepdims=True) * inv_n
    s2_ref[...] = jnp.broadcast_to(lax.rsqrt(var + EPS), s2_ref.shape)

    def ln_step(rc, carry):
        r0 = pl.multiple_of(rc * LN_ROWS, LN_ROWS)
        rows = pl.ds(r0, LN_ROWS)
        mu_b, rstd_b = s1_ref[rows, :], s2_ref[rows, :]
        for ct in range(lane_tiles):
            ls = slice(ct * V7X_LANES, (ct + 1) * V7X_LANES)
            yn = (y_ref[rows, ls] - mu_b) * rstd_b * g_ref[:, ls] + b_ref[:, ls]
            o_ref[rows, ls] = (yn * jax.nn.sigmoid(yn)).astype(o_ref.dtype)
        return carry

    lax.fori_loop(0, tt // LN_ROWS, ln_step, 0, unroll=2)


def _conv_module(c, c_meta, conv_w, conv_b, ln_g, ln_b, *, batch, seq, tt):
    rows, channels = c.shape
    nt = seq // tt
    halo_per_tile = tt // CONV_HALO
    return pl.pallas_call(
        functools.partial(_conv_kernel, tt=tt),
        out_shape=jax.ShapeDtypeStruct((rows, channels), BF16),
        grid=(batch, nt),
        in_specs=[pl.BlockSpec((tt, channels), lambda b, t: (b * nt + t, 0)),
                  pl.BlockSpec((CONV_HALO, channels),
                               lambda b, t: (jnp.maximum((b * nt + t) * halo_per_tile - 1, 0), 0)),
                  pl.BlockSpec((N_META, channels), lambda b, t: (0, 0)),
                  pl.BlockSpec((CONV_WIDTH, channels), lambda b, t: (0, 0)),
                  pl.BlockSpec((1, channels), lambda b, t: (0, 0)),
                  pl.BlockSpec((1, channels), lambda b, t: (0, 0)),
                  pl.BlockSpec((1, channels), lambda b, t: (0, 0))],
        out_specs=pl.BlockSpec((tt, channels), lambda b, t: (b * nt + t, 0)),
        scratch_shapes=[pltpu.VMEM((channels // V7X_LANES, tt + CONV_HALO, V7X_LANES), F32),
                        pltpu.VMEM((tt, channels), F32),
                        pltpu.VMEM((tt, V7X_LANES), F32),
                        pltpu.VMEM((tt, V7X_LANES), F32)],
        compiler_params=_params("arbitrary", "arbitrary"),
        name="conformer_conv",
    )(c, c, c_meta, conv_w, conv_b.reshape(1, channels), ln_g.reshape(1, channels), ln_b.reshape(1, channels))


def _merge_kernel(a_ref, c_ref, wa_ref, wc_ref, bc_ref, ga_ref, gb_ref, o_ref):
    a = jnp.dot(a_ref[...], wa_ref[...].astype(BF16), preferred_element_type=F32)
    b = jnp.dot(c_ref[...], wc_ref[...].astype(BF16), preferred_element_type=F32) + bc_ref[...]
    o_ref[...] = (ga_ref[...].astype(F32) * a + gb_ref[...].astype(F32) * b).astype(o_ref.dtype)


def _merge(attn, conv, w_ao, w_co, b_co, gates, *, tm, tn):
    rows, k = attn.shape
    d = w_ao.shape[1]
    nj = d // tn
    return pl.pallas_call(
        _merge_kernel,
        out_shape=jax.ShapeDtypeStruct((rows, d), BF16),
        grid=(rows // tm, nj),
        in_specs=[pl.BlockSpec((tm, k), lambda i, j: (i, 0)),
                  pl.BlockSpec((tm, k), lambda i, j: (i, 0)),
                  pl.BlockSpec((k, tn), lambda i, j: (0, j)),
                  pl.BlockSpec((k, tn), lambda i, j: (0, j)),
                  pl.BlockSpec((1, tn), lambda i, j: (0, j)),
                  pl.BlockSpec((tm, tn), lambda i, j: (i, j)),
                  pl.BlockSpec((tm, tn), lambda i, j: (i, j + nj))],
        out_specs=pl.BlockSpec((tm, tn), lambda i, j: (i, j)),
        compiler_params=_params("arbitrary", "arbitrary"),
        name="gated_merge",
    )(attn, conv, w_ao, w_co, b_co.reshape(1, d), gates, gates)


def _residual_kernel(x_ref, w_ref, r_ref, o_ref):
    o_ref[...] = r_ref[...] + jnp.dot(x_ref[...], w_ref[...].astype(BF16), preferred_element_type=F32)


def _residual_matmul(x, w, resid, *, tm, tn, single_buffer_x=False):
    rows, k = x.shape
    n = w.shape[1]
    x_spec = (pl.BlockSpec((tm, k), lambda i, j: (i, 0), pipeline_mode=pl.Buffered(1))
              if single_buffer_x else pl.BlockSpec((tm, k), lambda i, j: (i, 0)))
    return pl.pallas_call(
        _residual_kernel,
        out_shape=jax.ShapeDtypeStruct((rows, n), F32),
        grid=(rows // tm, n // tn),
        in_specs=[x_spec,
                  pl.BlockSpec((k, tn), lambda i, j: (0, j)),
                  pl.BlockSpec((tm, tn), lambda i, j: (i, j))],
        out_specs=pl.BlockSpec((tm, tn), lambda i, j: (i, j)),
        compiler_params=_params("arbitrary", "arbitrary"),
        name="residual_matmul",
    )(x, w, resid)


def _rope_tables(first_pos, length):
    pos = jnp.arange(first_pos, first_pos + length, dtype=F32)
    inv_freq = ROPE_THETA ** (-jnp.arange(0, HEAD_DIM, 2, dtype=F32) / HEAD_DIM)
    ang = pos[:, None] * inv_freq[None, :]
    cos, sin = jnp.cos(ang), jnp.sin(ang)
    reps = V7X_LANES // HEAD_DIM
    return (jnp.tile(jnp.concatenate([cos, cos], axis=1), (1, reps)),
            jnp.tile(jnp.concatenate([-sin, sin], axis=1), (1, reps)))


def kernel(x, meta_tokens, mix_norm_g, w_in, b_in, attn_sinks, conv_w, conv_b, conv_ln_g, conv_ln_b,
           w_attn_o, w_conv_o, b_conv_o, w_out, ffn_norm_g, w_gate_up, w_down, final_norm_g):
    batch, seq, d = x.shape
    depth = w_in.shape[0]
    assert depth == 1, "one layer: the meta rows' outputs are never needed"
    conv_dim = conv_w.shape[-1]
    ffn = w_down.shape[1]
    in_dim = w_in.shape[-1]
    rows = batch * seq
    col_conv = Q_DIM + 2 * KV_DIM
    col_gate = col_conv + 2 * conv_dim
    assert in_dim == col_gate + 2 * d

    w_in2 = w_in.reshape(d, in_dim)
    b_in2 = b_in.reshape(1, in_dim)
    h0 = x.reshape(rows, d)
    tm = 1024
    qkv_tn = 512

    u = _rmsnorm(h0, mix_norm_g[0], BF16, tm=256)
    u_meta = _rmsnorm(meta_tokens, mix_norm_g[0], BF16, tm=N_META)

    cos, sin_signed = _rope_tables(N_META, seq)
    cos_m, sin_m = _rope_tables(0, N_META)
    qkv = _qkv_proj(u, w_in2, b_in2, cos, sin_signed, tm=tm, tn=qkv_tn, j0=0,
                    nj=(Q_DIM + 2 * KV_DIM) // qkv_tn, rows_per_seq=seq)
    kv_meta = _qkv_proj(u_meta, w_in2, b_in2, cos_m, sin_m, tm=N_META, tn=qkv_tn, j0=Q_DIM // qkv_tn,
                        nj=1, rows_per_seq=N_META)
    glu = functools.partial(_pair_proj, w=w_in2, b=b_in2, col_a=col_conv, col_b=col_conv + conv_dim,
                            width=conv_dim, tn=256, combine=_glu, out_dtype=F32, name="conv_glu_proj")
    c = glu(u, tm=tm)
    c_meta = glu(u_meta, tm=N_META)
    gates = _gates_proj(u, w_in2, b_in2, col0=col_gate, width=2 * d, tm=tm, tn=512)

    attn = _attention(qkv, kv_meta, attn_sinks[0], batch=batch, seq=seq)
    conv = _conv_module(c, c_meta, conv_w.reshape(CONV_WIDTH, conv_dim), conv_b[0], conv_ln_g[0],
                        conv_ln_b[0], batch=batch, seq=seq, tt=256)

    merged = _merge(attn, conv, w_attn_o[0], w_conv_o[0], b_conv_o[0], gates, tm=tm, tn=512)
    h1 = _residual_matmul(merged, w_out[0], h0, tm=tm, tn=512)

    u2 = _rmsnorm(h1, ffn_norm_g[0], BF16, tm=256)
    act = _pair_proj(u2, w=w_gate_up[0], b=None, col_a=0, col_b=ffn, width=ffn,
                     tm=tm, tn=256, combine=_swiglu, out_dtype=BF16, name="ffn_gate_up")
    h2 = _residual_matmul(act, w_down[0], h1, tm=tm, tn=256, single_buffer_x=True)

    y = _rmsnorm(h2, final_norm_g, x.dtype, tm=256)
    return y.reshape(batch, seq, d)
```

```python
import functools

import jax
import jax.numpy as jnp
from jax import lax
from jax.experimental import pallas as pl
from jax.experimental.pallas import tpu as pltpu

N_META = 16
HEAD_DIM = 64
N_Q_HEADS = 32
N_KV_HEADS = 4
GROUP = N_Q_HEADS // N_KV_HEADS
WINDOW = 128
Q_DIM = N_Q_HEADS * HEAD_DIM
KV_DIM = N_KV_HEADS * HEAD_DIM
CONV_WIDTH = 31
ROPE_THETA = 10000.0
EPS = 1e-6

V7X_LANES = 128
V7X_BF16_ROWS = 16
V7X_VMEM_BYTES = 64 * 1024 * 1024
MIB = 1024 * 1024

F32 = jnp.float32
BF16 = jnp.bfloat16


def _params(vmem_mib, *sem):
    assert vmem_mib * MIB < V7X_VMEM_BYTES
    return pltpu.CompilerParams(dimension_semantics=sem, vmem_limit_bytes=vmem_mib * MIB)


def _after(x, dep):
    if dep is None:
        return x
    zero = lax.shift_right_logical(
        lax.shift_right_logical(lax.bitcast_convert_type(dep, jnp.uint32), jnp.uint32(16)), jnp.uint32(16))
    return lax.bitcast_convert_type(lax.bitcast_convert_type(x, jnp.uint32) | zero, x.dtype)


def _rmsnorm_kernel(x_ref, g_ref, o_ref):
    x = x_ref[...]
    ms = jnp.mean(x * x, axis=-1, keepdims=True)
    o_ref[...] = (x * lax.rsqrt(ms + EPS) * g_ref[...]).astype(o_ref.dtype)


def _rmsnorm(x, g, out_dtype, tm):
    rows, d = x.shape
    return pl.pallas_call(
        _rmsnorm_kernel,
        out_shape=jax.ShapeDtypeStruct((rows, d), out_dtype),
        grid=(rows // tm,),
        in_specs=[pl.BlockSpec((tm, d), lambda i: (i, 0)),
                  pl.BlockSpec((1, d), lambda i: (0, 0))],
        out_specs=pl.BlockSpec((tm, d), lambda i: (i, 0)),
        compiler_params=_params(32, "arbitrary"),
        name="rmsnorm",
    )(x, g.reshape(1, d))


def _rmsnorm_chunk(x_ref, g_ref, store):
    dep = None
    for r0 in range(0, x_ref.shape[0], V7X_BF16_ROWS):
        x = x_ref[r0:r0 + V7X_BF16_ROWS, :]
        x = jnp.concatenate([_after(x[:, :V7X_LANES], dep), x[:, V7X_LANES:]], axis=1)
        ms = jnp.mean(x * x, axis=-1, keepdims=True)
        y = x * lax.rsqrt(ms + EPS) * g_ref[...]
        store(r0, y.astype(BF16))
        dep = y[:, :V7X_LANES]


def _next_tile_norm_maps(n_tiles, chunks):
    last = n_tiles * chunks - 1

    def chunk_map(i, j):
        return (jnp.where(i >= n_tiles, last, i * chunks + jnp.minimum(j, chunks - 1)), 0)

    def w_col(i, j):
        return jnp.where(i == 0, 0, j)

    def out_row(i):
        return jnp.maximum(i - 1, 0)

    return chunk_map, w_col, out_row


def _next_tile_norm_body(i, j, n_tiles, chunks, norm, matmul):
    do_norm = jnp.logical_and(j < chunks, i < n_tiles)

    @pl.when(jnp.logical_and(i == 0, do_norm))
    def _():
        norm(0)

    for slot in (0, 1):
        mine = jnp.logical_and(i > 0, i % 2 == slot)

        @pl.when(jnp.logical_and(mine, do_norm))
        def _():
            norm(slot)
            matmul(1 - slot)

        @pl.when(jnp.logical_and(mine, jnp.logical_not(do_norm)))
        def _():
            matmul(1 - slot)


def _rope(z, cos, sin_signed):
    lane = lax.broadcasted_iota(jnp.int32, (z.shape[0], V7X_LANES), 1)
    first_half = (lane & (HEAD_DIM - 1)) < HEAD_DIM // 2
    out = []
    for t in range(z.shape[1] // V7X_LANES):
        zt = z[:, t * V7X_LANES:(t + 1) * V7X_LANES]
        partner = jnp.where(first_half, pltpu.roll(zt, V7X_LANES - HEAD_DIM // 2, 1),
                            pltpu.roll(zt, HEAD_DIM // 2, 1))
        out.append(zt * cos + partner * sin_signed)
    return jnp.concatenate(out, axis=1)


def _qkv_epilogue(z, j, tn, cos, sin_signed):
    roped = _rope(z, cos, sin_signed)
    lane = lax.broadcasted_iota(jnp.int32, z.shape, 1)
    is_q = j < Q_DIM // tn
    use_rope = jnp.logical_or(is_q, lane < KV_DIM)
    scale = jnp.where(is_q, HEAD_DIM ** -0.5, 1.0).astype(F32)
    return (jnp.where(use_rope, roped, z) * scale).astype(BF16)


def _qkv_kernel(u_ref, w_ref, b_ref, cos_ref, sin_ref, o_ref, *, tn, j0):
    z = jnp.dot(u_ref[...], w_ref[...].astype(BF16), preferred_element_type=F32) + b_ref[...]
    o_ref[...] = _qkv_epilogue(z, pl.program_id(1) + j0, tn, cos_ref[...], sin_ref[...])


def _qkv_proj(u, w_in, b_in, cos, sin_signed, *, tm, tn, j0, nj):
    rows, d = u.shape
    assert rows == tm
    return pl.pallas_call(
        functools.partial(_qkv_kernel, tn=tn, j0=j0),
        out_shape=jax.ShapeDtypeStruct((rows, nj * tn), BF16),
        grid=(1, nj),
        in_specs=[pl.BlockSpec((tm, d), lambda i, j: (0, 0)),
                  pl.BlockSpec((d, tn), lambda i, j: (0, j + j0)),
                  pl.BlockSpec((1, tn), lambda i, j: (0, j + j0)),
                  pl.BlockSpec((tm, V7X_LANES), lambda i, j: (0, 0)),
                  pl.BlockSpec((tm, V7X_LANES), lambda i, j: (0, 0))],
        out_specs=pl.BlockSpec((tm, tn), lambda i, j: (0, j)),
        compiler_params=_params(32, "arbitrary", "arbitrary"),
        name="qkv_proj_meta",
    )(u, w_in, b_in, cos, sin_signed)


def _qkv_norm_kernel(x_ref, g_ref, w_ref, b_ref, cos_ref, sin_ref, o_ref, uo_ref, u0_ref, u1_ref,
                     *, tn, n_tiles, chunks):
    i, j = pl.program_id(0), pl.program_id(1)
    chunk = x_ref.shape[0]
    slots = (u0_ref, u1_ref)

    def norm(slot):
        base = pl.multiple_of(jnp.minimum(j, chunks - 1) * chunk, chunk)

        def store(r0, rows):
            slots[slot][pl.ds(pl.multiple_of(base + r0, V7X_BF16_ROWS), V7X_BF16_ROWS), :] = rows
            uo_ref[r0:r0 + V7X_BF16_ROWS, :] = rows

        _rmsnorm_chunk(x_ref, g_ref, store)

    def matmul(slot):
        z = jnp.dot(slots[slot][...], w_ref[...].astype(BF16), preferred_element_type=F32) + b_ref[...]
        o_ref[...] = _qkv_epilogue(z, j, tn, cos_ref[...], sin_ref[...])

    _next_tile_norm_body(i, j, n_tiles, chunks, norm, matmul)


def _qkv_norm_proj(x, g, w_in, b_in, cos, sin_signed, *, tm, tn, nj, chunks, rows_per_seq):
    rows, d = x.shape
    n_tiles, seq_tiles, chunk = rows // tm, rows_per_seq // tm, tm // chunks
    chunk_map, w_col, out_row = _next_tile_norm_maps(n_tiles, chunks)
    return pl.pallas_call(
        functools.partial(_qkv_norm_kernel, tn=tn, n_tiles=n_tiles, chunks=chunks),
        out_shape=(jax.ShapeDtypeStruct((rows, nj * tn), BF16), jax.ShapeDtypeStruct((rows, d), BF16)),
        grid=(n_tiles + 1, nj),
        in_specs=[pl.BlockSpec((chunk, d), chunk_map),
                  pl.BlockSpec((1, d), lambda i, j: (0, 0)),
                  pl.BlockSpec((d, tn), lambda i, j: (0, w_col(i, j))),
                  pl.BlockSpec((1, tn), lambda i, j: (0, w_col(i, j))),
                  pl.BlockSpec((tm, V7X_LANES), lambda i, j: (out_row(i) % seq_tiles, 0)),
                  pl.BlockSpec((tm, V7X_LANES), lambda i, j: (out_row(i) % seq_tiles, 0))],
        out_specs=(pl.BlockSpec((tm, tn), lambda i, j: (out_row(i), w_col(i, j))),
                   pl.BlockSpec((chunk, d), chunk_map)),
        scratch_shapes=[pltpu.VMEM((tm, d), BF16), pltpu.VMEM((tm, d), BF16)],
        compiler_params=_params(58, "arbitrary", "arbitrary"),
        name="qkv_proj",
    )(x, g.reshape(1, d), w_in, b_in, cos, sin_signed)


def _glu(a, g):
    return a * jax.nn.sigmoid(g)


def _swiglu(g, up):
    return g * jax.nn.sigmoid(g) * up


def _pair_kernel(u_ref, wa_ref, wb_ref, ba_ref, bb_ref, o_ref, *, combine):
    a = jnp.dot(u_ref[...], wa_ref[...].astype(BF16), preferred_element_type=F32) + ba_ref[...]
    b = jnp.dot(u_ref[...], wb_ref[...].astype(BF16), preferred_element_type=F32) + bb_ref[...]
    o_ref[...] = combine(a, b).astype(o_ref.dtype)


def _pair_proj(u, w, b, *, col_a, col_b, width, tm, tn, combine, out_dtype, name):
    rows, d = u.shape
    ja, jb = col_a // tn, col_b // tn
    assert ja * tn == col_a and jb * tn == col_b and width % tn == 0
    return pl.pallas_call(
        functools.partial(_pair_kernel, combine=combine),
        out_shape=jax.ShapeDtypeStruct((rows, width), out_dtype),
        grid=(rows // tm, width // tn),
        in_specs=[pl.BlockSpec((tm, d), lambda i, j: (i, 0)),
                  pl.BlockSpec((d, tn), lambda i, j: (0, j + ja)),
                  pl.BlockSpec((d, tn), lambda i, j: (0, j + jb)),
                  pl.BlockSpec((1, tn), lambda i, j: (0, j + ja)),
                  pl.BlockSpec((1, tn), lambda i, j: (0, j + jb))],
        out_specs=pl.BlockSpec((tm, tn), lambda i, j: (i, j)),
        compiler_params=_params(48, "arbitrary", "arbitrary"),
        name=name,
    )(u, w, w, b, b)


CONV_HALO = 32
CONV_ROWS = 8
CONV_CHAINS = 2


def _conv_chunk(first, c_ref, cm_ref, w_ref, cb_ref, g_ref, b_ref, o_ref, ext_ref, y_ref):
    lane_tiles, rows = ext_ref.shape[0], c_ref.shape[0]
    lead = CONV_HALO - (CONV_WIDTH - 1)
    groups = rows // CONV_ROWS

    @pl.when(first)
    def _():
        for ct in range(lane_tiles):
            ext_ref[ct, 0:CONV_HALO - N_META, :] = jnp.zeros((CONV_HALO - N_META, V7X_LANES), F32)
            ext_ref[ct, CONV_HALO - N_META:CONV_HALO, :] = cm_ref[:, ct * V7X_LANES:(ct + 1) * V7X_LANES]

    @pl.when(jnp.logical_not(first))
    def _():
        for ct in range(lane_tiles):
            ext_ref[ct, 0:CONV_HALO, :] = ext_ref[ct, rows:rows + CONV_HALO, :]

    for ct in range(lane_tiles):
        ext_ref[ct, CONV_HALO:, :] = c_ref[:, ct * V7X_LANES:(ct + 1) * V7X_LANES]

    prev = [None] * CONV_CHAINS
    row_sum = [None] * groups
    n = 0
    for ct in range(lane_tiles):
        ls = slice(ct * V7X_LANES, (ct + 1) * V7X_LANES)
        for j in range(groups):
            acc = _after(jnp.broadcast_to(cb_ref[:, ls], (CONV_ROWS, V7X_LANES)), prev[n % CONV_CHAINS])
            for k in range(CONV_WIDTH):
                tap = jnp.broadcast_to(w_ref[k:k + 1, ls], (CONV_ROWS, V7X_LANES))
                r0 = j * CONV_ROWS + lead + k
                acc = acc + ext_ref[ct, r0:r0 + CONV_ROWS, :] * tap
            y_ref[j * CONV_ROWS:(j + 1) * CONV_ROWS, ls] = acc
            row_sum[j] = acc if row_sum[j] is None else row_sum[j] + acc
            prev[n % CONV_CHAINS] = acc
            n += 1

    inv_n = 1.0 / (lane_tiles * V7X_LANES)
    for j in range(groups):
        rs = slice(j * CONV_ROWS, (j + 1) * CONV_ROWS)
        mu = jnp.sum(row_sum[j], axis=1, keepdims=True) * inv_n
        parts = [None, None]
        for ct in range(lane_tiles):
            d = y_ref[rs, ct * V7X_LANES:(ct + 1) * V7X_LANES] - mu
            parts[ct % 2] = d * d if parts[ct % 2] is None else parts[ct % 2] + d * d
        var = jnp.sum(parts[0] + parts[1], axis=1, keepdims=True) * inv_n
        rstd = lax.rsqrt(var + EPS)
        for ct in range(lane_tiles):
            ls = slice(ct * V7X_LANES, (ct + 1) * V7X_LANES)
            yn = (y_ref[rs, ls] - mu) * rstd * g_ref[:, ls] + b_ref[:, ls]
            y_ref[rs, ls] = yn * jax.nn.sigmoid(yn)
    o_ref[...] = y_ref[...].astype(o_ref.dtype)


def _gates_conv_kernel(u_ref, w_ref, b_ref, c_ref, cm_ref, cw_ref, cb_ref, g_ref, bb_ref,
                       o_ref, co_ref, ext_ref, y_ref, *, seq_tiles):
    i, j = pl.program_id(0), pl.program_id(1)
    first = jnp.logical_and(i % seq_tiles == 0, j == 0)
    _conv_chunk(first, c_ref, cm_ref, cw_ref, cb_ref, g_ref, bb_ref, co_ref, ext_ref, y_ref)
    z = jnp.dot(u_ref[...], w_ref[...].astype(BF16), preferred_element_type=F32) + b_ref[...]
    o_ref[...] = jax.nn.sigmoid(z).astype(o_ref.dtype)


def _gates_conv(u, w, b, c, c_meta, conv_w, conv_b, ln_g, ln_b, *, col0, width, tm, tn, rows_per_seq):
    rows, d = u.shape
    channels = c.shape[1]
    j0, nj = col0 // tn, width // tn
    assert j0 * tn == col0 and nj * tn == width and tm % nj == 0
    chunk = tm // nj
    assert chunk % V7X_BF16_ROWS == 0 and chunk >= CONV_HALO
    small = lambda i, j: (0, 0)
    return pl.pallas_call(
        functools.partial(_gates_conv_kernel, seq_tiles=rows_per_seq // tm),
        out_shape=(jax.ShapeDtypeStruct((rows, width), BF16), jax.ShapeDtypeStruct((rows, channels), BF16)),
        grid=(rows // tm, nj),
        in_specs=[pl.BlockSpec((tm, d), lambda i, j: (i, 0)),
                  pl.BlockSpec((d, tn), lambda i, j: (0, j + j0)),
                  pl.BlockSpec((1, tn), lambda i, j: (0, j + j0)),
                  pl.BlockSpec((chunk, channels), lambda i, j: (i * nj + j, 0)),
                  pl.BlockSpec((N_META, channels), small),
                  pl.BlockSpec((CONV_WIDTH, channels), small),
                  pl.BlockSpec((1, channels), small),
                  pl.BlockSpec((1, channels), small),
                  pl.BlockSpec((1, channels), small)],
        out_specs=(pl.BlockSpec((tm, tn), lambda i, j: (i, j)),
                   pl.BlockSpec((chunk, channels), lambda i, j: (i * nj + j, 0))),
        scratch_shapes=[pltpu.VMEM((channels // V7X_LANES, chunk + CONV_HALO, V7X_LANES), F32),
                        pltpu.VMEM((chunk, channels), F32)],
        compiler_params=_params(48, "arbitrary", "arbitrary"),
        name="gates_conv",
    )(u, w, b, c, c_meta, conv_w, conv_b.reshape(1, channels), ln_g.reshape(1, channels),
      ln_b.reshape(1, channels))


ATT_KEYS = 3 * WINDOW
PAIRS = GROUP // 2


def _attn_kernel(sink_ref, q_ref, kc_ref, kp_ref, vc_ref, vp_ref, kvm_ref, o_ref):
    first_block = pl.program_id(1) == 0
    rows = PAIRS * WINDOW
    r = lax.broadcasted_iota(jnp.int32, (rows, ATT_KEYS), 0) & (WINDOW - 1)
    c = lax.broadcasted_iota(jnp.int32, (rows, ATT_KEYS), 1)
    c_min = jnp.where(first_block, WINDOW, 0)
    band = (c > r) & (c <= r + WINDOW) & (c >= c_min)
    meta = (c >= 2 * WINDOW) & (c < 2 * WINDOW + N_META)
    bias = jnp.where(band | meta, 0.0, -jnp.inf).astype(F32)

    row_pair = lax.shift_right_logical(lax.broadcasted_iota(jnp.int32, (rows, 1), 0), WINDOW.bit_length() - 1)
    lane = lax.broadcasted_iota(jnp.int32, (ATT_KEYS, V7X_LANES), 1)
    low = lane < HEAD_DIM
    zero_pad = jnp.zeros((WINDOW - N_META, V7X_LANES), F32)

    for g in range(N_KV_HEADS):
        tile, upper = g // 2, g % 2
        sl = slice(tile * V7X_LANES, (tile + 1) * V7X_LANES)

        def halves(prev_ref, cur_ref, meta_cols):
            x = jnp.concatenate([prev_ref[:, sl].astype(F32), cur_ref[:, sl].astype(F32),
                                 kvm_ref[:, meta_cols].astype(F32), zero_pad], axis=0)
            swapped = pltpu.roll(x, HEAD_DIM, 1)
            lo_src, hi_src = (swapped, x) if upper else (x, swapped)
            return (jnp.where(low, lo_src, 0.0).astype(BF16), jnp.where(low, 0.0, hi_src).astype(BF16))

        k_even, k_odd = halves(kp_ref, kc_ref, slice(tile * V7X_LANES, (tile + 1) * V7X_LANES))
        v_even, v_odd = halves(vp_ref, vc_ref, slice(KV_DIM + tile * V7X_LANES, KV_DIM + (tile + 1) * V7X_LANES))

        qp = jnp.concatenate(
            [q_ref[:, g * GROUP * HEAD_DIM + m * V7X_LANES: g * GROUP * HEAD_DIM + (m + 1) * V7X_LANES]
             for m in range(PAIRS)], axis=0)

        out = None
        for parity, (kz, vz) in enumerate(((k_even, v_even), (k_odd, v_odd))):
            sink = jnp.zeros((rows, 1), F32)
            for m in range(PAIRS):
                sink = jnp.where(row_pair == m, sink_ref[g * GROUP + 2 * m + parity], sink)
            s = lax.dot_general(qp, kz, (((1,), (1,)), ((), ())), preferred_element_type=F32) + bias
            mx = jnp.maximum(jnp.max(s, axis=1, keepdims=True), sink)
            p = jnp.exp(s - mx)
            denom = jnp.sum(p, axis=1, keepdims=True) + jnp.exp(sink - mx)
            o = jnp.dot(p.astype(BF16), vz, preferred_element_type=F32) * (1.0 / denom)
            out = o if out is None else out + o
        for m in range(PAIRS):
            col = g * GROUP * HEAD_DIM + m * V7X_LANES
            o_ref[:, col:col + V7X_LANES] = out[m * WINDOW:(m + 1) * WINDOW, :].astype(o_ref.dtype)


def _attention(qkv, kv_meta, sinks, *, batch, seq):
    nb = seq // WINDOW
    kcol, vcol = Q_DIM // KV_DIM, Q_DIM // KV_DIM + 1

    def cur(col):
        return lambda b, j: (b * nb + j, col)

    def prev(col):
        return lambda b, j: (jnp.maximum(b * nb + j - 1, 0), col)

    return pl.pallas_call(
        _attn_kernel,
        out_shape=jax.ShapeDtypeStruct((batch * seq, Q_DIM), BF16),
        grid=(batch, nb),
        in_specs=[pl.BlockSpec(memory_space=pltpu.SMEM),
                  pl.BlockSpec((WINDOW, Q_DIM), lambda b, j: (b * nb + j, 0)),
                  pl.BlockSpec((WINDOW, KV_DIM), cur(kcol)),
                  pl.BlockSpec((WINDOW, KV_DIM), prev(kcol)),
                  pl.BlockSpec((WINDOW, KV_DIM), cur(vcol)),
                  pl.BlockSpec((WINDOW, KV_DIM), prev(vcol)),
                  pl.BlockSpec((N_META, 2 * KV_DIM), lambda b, j: (0, 0))],
        out_specs=pl.BlockSpec((WINDOW, Q_DIM), lambda b, j: (b * nb + j, 0)),
        compiler_params=_params(32, "arbitrary", "arbitrary"),
        name="swa_attention",
    )(sinks, qkv, qkv, qkv, qkv, qkv, kv_meta)


def _merge_kernel(a_ref, c_ref, wa_ref, wc_ref, bc_ref, ga_ref, gb_ref, o_ref):
    a = jnp.dot(a_ref[...], wa_ref[...].astype(BF16), preferred_element_type=F32)
    b = jnp.dot(c_ref[...], wc_ref[...].astype(BF16), preferred_element_type=F32) + bc_ref[...]
    o_ref[...] = (ga_ref[...].astype(F32) * a + gb_ref[...].astype(F32) * b).astype(o_ref.dtype)


def _merge(attn, conv, w_ao, w_co, b_co, gates, *, tm, tn):
    rows, k = attn.shape
    d = w_ao.shape[1]
    nj = d // tn
    return pl.pallas_call(
        _merge_kernel,
        out_shape=jax.ShapeDtypeStruct((rows, d), BF16),
        grid=(rows // tm, nj),
        in_specs=[pl.BlockSpec((tm, k), lambda i, j: (i, 0)),
                  pl.BlockSpec((tm, k), lambda i, j: (i, 0)),
                  pl.BlockSpec((k, tn), lambda i, j: (0, j)),
                  pl.BlockSpec((k, tn), lambda i, j: (0, j)),
                  pl.BlockSpec((1, tn), lambda i, j: (0, j)),
                  pl.BlockSpec((tm, tn), lambda i, j: (i, j)),
                  pl.BlockSpec((tm, tn), lambda i, j: (i, j + nj))],
        out_specs=pl.BlockSpec((tm, tn), lambda i, j: (i, j)),
        compiler_params=_params(48, "arbitrary", "arbitrary"),
        name="gated_merge",
    )(attn, conv, w_ao, w_co, b_co.reshape(1, d), gates, gates)


def _residual_kernel(x_ref, w_ref, r_ref, o_ref):
    o_ref[...] = r_ref[...] + jnp.dot(x_ref[...], w_ref[...].astype(BF16), preferred_element_type=F32)


def _residual_matmul(x, w, resid, *, tm, tn, vmem_mib, single_buffer_x=False):
    rows, k = x.shape
    n = w.shape[1]
    x_spec = (pl.BlockSpec((tm, k), lambda i, j: (i, 0), pipeline_mode=pl.Buffered(1))
              if single_buffer_x else pl.BlockSpec((tm, k), lambda i, j: (i, 0)))
    return pl.pallas_call(
        _residual_kernel,
        out_shape=jax.ShapeDtypeStruct((rows, n), F32),
        grid=(rows // tm, n // tn),
        in_specs=[x_spec,
                  pl.BlockSpec((k, tn), lambda i, j: (0, j)),
                  pl.BlockSpec((tm, tn), lambda i, j: (i, j))],
        out_specs=pl.BlockSpec((tm, tn), lambda i, j: (i, j)),
        compiler_params=_params(vmem_mib, "arbitrary", "arbitrary"),
        name="residual_matmul",
    )(x, w, resid)


def _ffn_up_kernel(h_ref, g_ref, wa_ref, wb_ref, o_ref, u0_ref, u1_ref, *, n_tiles, chunks):
    i, j = pl.program_id(0), pl.program_id(1)
    chunk = h_ref.shape[0]
    slots = (u0_ref, u1_ref)

    def norm(slot):
        base = pl.multiple_of(jnp.minimum(j, chunks - 1) * chunk, chunk)

        def store(r0, rows):
            slots[slot][pl.ds(pl.multiple_of(base + r0, V7X_BF16_ROWS), V7X_BF16_ROWS), :] = rows

        _rmsnorm_chunk(h_ref, g_ref, store)

    def matmul(slot):
        u = slots[slot][...]
        a = jnp.dot(u, wa_ref[...].astype(BF16), preferred_element_type=F32)
        b = jnp.dot(u, wb_ref[...].astype(BF16), preferred_element_type=F32)
        o_ref[...] = _swiglu(a, b).astype(o_ref.dtype)

    _next_tile_norm_body(i, j, n_tiles, chunks, norm, matmul)


def _ffn_up(h, g, w, *, ffn, tm, tn, chunks):
    rows, d = h.shape
    n_tiles, nj, chunk = rows // tm, ffn // tn, tm // chunks
    assert nj * tn == ffn and nj >= chunks
    chunk_map, w_col, out_row = _next_tile_norm_maps(n_tiles, chunks)
    return pl.pallas_call(
        functools.partial(_ffn_up_kernel, n_tiles=n_tiles, chunks=chunks),
        out_shape=jax.ShapeDtypeStruct((rows, ffn), BF16),
        grid=(n_tiles + 1, nj),
        in_specs=[pl.BlockSpec((chunk, d), chunk_map),
                  pl.BlockSpec((1, d), lambda i, j: (0, 0)),
                  pl.BlockSpec((d, tn), lambda i, j: (0, w_col(i, j))),
                  pl.BlockSpec((d, tn), lambda i, j: (0, w_col(i, j) + nj))],
        out_specs=pl.BlockSpec((tm, tn), lambda i, j: (out_row(i), w_col(i, j))),
        scratch_shapes=[pltpu.VMEM((tm, d), BF16), pltpu.VMEM((tm, d), BF16)],
        compiler_params=_params(48, "arbitrary", "arbitrary"),
        name="ffn_gate_up",
    )(h, g.reshape(1, d), w, w)


def _rope_tables(first_pos, length):
    pos = jnp.arange(first_pos, first_pos + length, dtype=F32)
    inv_freq = ROPE_THETA ** (-jnp.arange(0, HEAD_DIM, 2, dtype=F32) / HEAD_DIM)
    ang = pos[:, None] * inv_freq[None, :]
    cos, sin = jnp.cos(ang), jnp.sin(ang)
    reps = V7X_LANES // HEAD_DIM
    return (jnp.tile(jnp.concatenate([cos, cos], axis=1), (1, reps)),
            jnp.tile(jnp.concatenate([-sin, sin], axis=1), (1, reps)))


def kernel(x, meta_tokens, mix_norm_g, w_in, b_in, attn_sinks, conv_w, conv_b, conv_ln_g, conv_ln_b,
           w_attn_o, w_conv_o, b_conv_o, w_out, ffn_norm_g, w_gate_up, w_down, final_norm_g):
    batch, seq, d = x.shape
    depth = w_in.shape[0]
    assert depth == 1, "one layer: the meta rows' outputs are never needed"
    conv_dim = conv_w.shape[-1]
    ffn = w_down.shape[1]
    in_dim = w_in.shape[-1]
    rows = batch * seq
    col_conv = Q_DIM + 2 * KV_DIM
    col_gate = col_conv + 2 * conv_dim
    assert in_dim == col_gate + 2 * d

    w_in2 = w_in.reshape(d, in_dim)
    b_in2 = b_in.reshape(1, in_dim)
    h0 = x.reshape(rows, d)
    tm = 1024
    qkv_tn = 512
    qkv_nj = (Q_DIM + 2 * KV_DIM) // qkv_tn

    cos, sin_signed = _rope_tables(N_META, seq)
    cos_m, sin_m = _rope_tables(0, N_META)

    qkv, u = _qkv_norm_proj(h0, mix_norm_g[0], w_in2, b_in2, cos, sin_signed, tm=tm, tn=qkv_tn,
                            nj=qkv_nj, chunks=qkv_nj - 1, rows_per_seq=seq)
    u_meta = _rmsnorm(meta_tokens, mix_norm_g[0], BF16, tm=N_META)
    kv_meta = _qkv_proj(u_meta, w_in2, b_in2, cos_m, sin_m, tm=N_META, tn=qkv_tn, j0=Q_DIM // qkv_tn, nj=1)
    glu = functools.partial(_pair_proj, w=w_in2, b=b_in2, col_a=col_conv, col_b=col_conv + conv_dim,
                            width=conv_dim, tn=256, combine=_glu, out_dtype=F32, name="conv_glu_proj")
    c = glu(u, tm=tm)
    c_meta = glu(u_meta, tm=N_META)

    gates, conv = _gates_conv(u, w_in2, b_in2, c, c_meta, conv_w.reshape(CONV_WIDTH, conv_dim), conv_b[0],
                              conv_ln_g[0], conv_ln_b[0], col0=col_gate, width=2 * d, tm=tm, tn=512,
                              rows_per_seq=seq)
    attn = _attention(qkv, kv_meta, attn_sinks[0], batch=batch, seq=seq)

    merged = _merge(attn, conv, w_attn_o[0], w_conv_o[0], b_conv_o[0], gates, tm=tm, tn=512)
    h1 = _residual_matmul(merged, w_out[0], h0, tm=tm, tn=512, vmem_mib=48)

    act = _ffn_up(h1, ffn_norm_g[0], w_gate_up[0], ffn=ffn, tm=tm, tn=256, chunks=16)
    h2 = _residual_matmul(act, w_down[0], h1, tm=tm, tn=256, vmem_mib=58, single_buffer_x=True)

    y = _rmsnorm(h2, final_norm_g, x.dtype, tm=256)
    return y.reshape(batch, seq, d)
```

```python
import functools

import jax
import jax.numpy as jnp
from jax import lax
from jax.experimental import pallas as pl
from jax.experimental.pallas import tpu as pltpu

N_META = 16
HEAD_DIM = 64
N_Q_HEADS = 32
N_KV_HEADS = 4
GROUP = N_Q_HEADS // N_KV_HEADS
WINDOW = 128
Q_DIM = N_Q_HEADS * HEAD_DIM
KV_DIM = N_KV_HEADS * HEAD_DIM
CONV_WIDTH = 31
ROPE_THETA = 10000.0
EPS = 1e-6

V7X_LANES = 128
V7X_BF16_ROWS = 16
V7X_VMEM_BYTES = 64 * 1024 * 1024
MXU_COLS = 256
MIB = 1024 * 1024

F32 = jnp.float32
BF16 = jnp.bfloat16


def _params(vmem_mib, *sem):
    assert vmem_mib * MIB < V7X_VMEM_BYTES
    return pltpu.CompilerParams(dimension_semantics=sem, vmem_limit_bytes=vmem_mib * MIB)


def _after(x, dep):
    if dep is None:
        return x
    zero = lax.shift_right_logical(
        lax.shift_right_logical(lax.bitcast_convert_type(dep, jnp.uint32), jnp.uint32(16)), jnp.uint32(16))
    return lax.bitcast_convert_type(lax.bitcast_convert_type(x, jnp.uint32) | zero, x.dtype)


def _rmsnorm_kernel(x_ref, g_ref, o_ref):
    x = x_ref[...]
    ms = jnp.mean(x * x, axis=-1, keepdims=True)
    o_ref[...] = (x * lax.rsqrt(ms + EPS) * g_ref[...]).astype(o_ref.dtype)


def _rmsnorm(x, g, out_dtype, tm):
    rows, d = x.shape
    return pl.pallas_call(
        _rmsnorm_kernel,
        out_shape=jax.ShapeDtypeStruct((rows, d), out_dtype),
        grid=(rows // tm,),
        in_specs=[pl.BlockSpec((tm, d), lambda i: (i, 0)),
                  pl.BlockSpec((1, d), lambda i: (0, 0))],
        out_specs=pl.BlockSpec((tm, d), lambda i: (i, 0)),
        compiler_params=_params(32, "arbitrary"),
        name="rmsnorm",
    )(x, g.reshape(1, d))


def _rmsnorm_chunk(x_ref, g_ref, store):
    dep = None
    for r0 in range(0, x_ref.shape[0], V7X_BF16_ROWS):
        x = x_ref[r0:r0 + V7X_BF16_ROWS, :]
        x = jnp.concatenate([_after(x[:, :V7X_LANES], dep), x[:, V7X_LANES:]], axis=1)
        ms = jnp.mean(x * x, axis=-1, keepdims=True)
        y = x * lax.rsqrt(ms + EPS) * g_ref[...]
        store(r0, y.astype(BF16))
        dep = y[:, :V7X_LANES]


def _next_tile_norm_maps(n_tiles, chunks):
    last = n_tiles * chunks - 1

    def chunk_map(i, j):
        return (jnp.where(i >= n_tiles, last, i * chunks + jnp.minimum(j, chunks - 1)), 0)

    def w_col(i, j):
        return jnp.where(i == 0, 0, j)

    def out_row(i):
        return jnp.maximum(i - 1, 0)

    return chunk_map, w_col, out_row


def _next_tile_norm_body(i, j, n_tiles, chunks, norm, matmul):
    do_norm = jnp.logical_and(j < chunks, i < n_tiles)

    @pl.when(jnp.logical_and(i == 0, do_norm))
    def _():
        norm(0)

    for slot in (0, 1):
        mine = jnp.logical_and(i > 0, i % 2 == slot)

        @pl.when(jnp.logical_and(mine, do_norm))
        def _():
            norm(slot)
            matmul(1 - slot)

        @pl.when(jnp.logical_and(mine, jnp.logical_not(do_norm)))
        def _():
            matmul(1 - slot)


def _rope(z, cos, sin_signed):
    lane = lax.broadcasted_iota(jnp.int32, (z.shape[0], V7X_LANES), 1)
    first_half = (lane & (HEAD_DIM - 1)) < HEAD_DIM // 2
    out = []
    for t in range(z.shape[1] // V7X_LANES):
        zt = z[:, t * V7X_LANES:(t + 1) * V7X_LANES]
        partner = jnp.where(first_half, pltpu.roll(zt, V7X_LANES - HEAD_DIM // 2, 1),
                            pltpu.roll(zt, HEAD_DIM // 2, 1))
        out.append(zt * cos + partner * sin_signed)
    return jnp.concatenate(out, axis=1)


def _qkv_epilogue(z, j, tn, cos, sin_signed):
    roped = _rope(z, cos, sin_signed)
    lane = lax.broadcasted_iota(jnp.int32, z.shape, 1)
    is_q = j < Q_DIM // tn
    use_rope = jnp.logical_or(is_q, lane < KV_DIM)
    scale = jnp.where(is_q, HEAD_DIM ** -0.5, 1.0).astype(F32)
    return (jnp.where(use_rope, roped, z) * scale).astype(BF16)


def _qkv_kernel(u_ref, w_ref, b_ref, cos_ref, sin_ref, o_ref, *, tn, j0):
    z = jnp.dot(u_ref[...], w_ref[...].astype(BF16), preferred_element_type=F32) + b_ref[...]
    o_ref[...] = _qkv_epilogue(z, pl.program_id(1) + j0, tn, cos_ref[...], sin_ref[...])


def _qkv_proj(u, w_in, b_in, cos, sin_signed, *, tm, tn, j0, nj):
    rows, d = u.shape
    assert rows == tm
    return pl.pallas_call(
        functools.partial(_qkv_kernel, tn=tn, j0=j0),
        out_shape=jax.ShapeDtypeStruct((rows, nj * tn), BF16),
        grid=(1, nj),
        in_specs=[pl.BlockSpec((tm, d), lambda i, j: (0, 0)),
                  pl.BlockSpec((d, tn), lambda i, j: (0, j + j0)),
                  pl.BlockSpec((1, tn), lambda i, j: (0, j + j0)),
                  pl.BlockSpec((tm, V7X_LANES), lambda i, j: (0, 0)),
                  pl.BlockSpec((tm, V7X_LANES), lambda i, j: (0, 0))],
        out_specs=pl.BlockSpec((tm, tn), lambda i, j: (0, j)),
        compiler_params=_params(32, "arbitrary", "arbitrary"),
        name="qkv_proj_meta",
    )(u, w_in, b_in, cos, sin_signed)


def _qkv_norm_kernel(x_ref, g_ref, w_ref, b_ref, cos_ref, sin_ref, o_ref, uo_ref, u0_ref, u1_ref,
                     *, tn, n_tiles, chunks):
    i, j = pl.program_id(0), pl.program_id(1)
    chunk = x_ref.shape[0]
    slots = (u0_ref, u1_ref)

    def norm(slot):
        base = pl.multiple_of(jnp.minimum(j, chunks - 1) * chunk, chunk)

        def store(r0, rows):
            slots[slot][pl.ds(pl.multiple_of(base + r0, V7X_BF16_ROWS), V7X_BF16_ROWS), :] = rows
            uo_ref[r0:r0 + V7X_BF16_ROWS, :] = rows

        _rmsnorm_chunk(x_ref, g_ref, store)

    def matmul(slot):
        z = jnp.dot(slots[slot][...], w_ref[...].astype(BF16), preferred_element_type=F32) + b_ref[...]
        o_ref[...] = _qkv_epilogue(z, j, tn, cos_ref[...], sin_ref[...])

    _next_tile_norm_body(i, j, n_tiles, chunks, norm, matmul)


def _qkv_norm_proj(x, g, w_in, b_in, cos, sin_signed, *, tm, tn, nj, chunks, rows_per_seq):
    rows, d = x.shape
    n_tiles, seq_tiles, chunk = rows // tm, rows_per_seq // tm, tm // chunks
    chunk_map, w_col, out_row = _next_tile_norm_maps(n_tiles, chunks)
    return pl.pallas_call(
        functools.partial(_qkv_norm_kernel, tn=tn, n_tiles=n_tiles, chunks=chunks),
        out_shape=(jax.ShapeDtypeStruct((rows, nj * tn), BF16), jax.ShapeDtypeStruct((rows, d), BF16)),
        grid=(n_tiles + 1, nj),
        in_specs=[pl.BlockSpec((chunk, d), chunk_map),
                  pl.BlockSpec((1, d), lambda i, j: (0, 0)),
                  pl.BlockSpec((d, tn), lambda i, j: (0, w_col(i, j))),
                  pl.BlockSpec((1, tn), lambda i, j: (0, w_col(i, j))),
                  pl.BlockSpec((tm, V7X_LANES), lambda i, j: (out_row(i) % seq_tiles, 0)),
                  pl.BlockSpec((tm, V7X_LANES), lambda i, j: (out_row(i) % seq_tiles, 0))],
        out_specs=(pl.BlockSpec((tm, tn), lambda i, j: (out_row(i), w_col(i, j))),
                   pl.BlockSpec((chunk, d), chunk_map)),
        scratch_shapes=[pltpu.VMEM((tm, d), BF16), pltpu.VMEM((tm, d), BF16)],
        compiler_params=_params(58, "arbitrary", "arbitrary"),
        name="qkv_proj",
    )(x, g.reshape(1, d), w_in, b_in, cos, sin_signed)


def _glu(a, g):
    return a * jax.nn.sigmoid(g)


def _swiglu(g, up):
    return g * jax.nn.sigmoid(g) * up


def _side_cast_specs(weight, block_rows, nj):
    w_rows, w_cols = weight.shape
    assert w_rows % block_rows == 0 and block_rows % V7X_BF16_ROWS == 0
    last = w_rows // block_rows - 1
    spec = pl.BlockSpec((block_rows, w_cols), lambda i, j: (jnp.minimum(i * nj + j, last), 0))
    return spec, jax.ShapeDtypeStruct((w_rows, w_cols), BF16), last + 1


def _pair_kernel(u_ref, wa_ref, wb_ref, ba_ref, bb_ref, *rest, combine):
    if len(rest) == 3:
        cast_in_ref, o_ref, cast_out_ref = rest
        cast_out_ref[...] = cast_in_ref[...].astype(BF16)
    else:
        (o_ref,) = rest
    a = jnp.dot(u_ref[...], wa_ref[...].astype(BF16), preferred_element_type=F32) + ba_ref[...]
    b = jnp.dot(u_ref[...], wb_ref[...].astype(BF16), preferred_element_type=F32) + bb_ref[...]
    o_ref[...] = combine(a, b).astype(o_ref.dtype)


def _pair_proj(u, w, b, *, col_a, col_b, width, tm, tn, combine, out_dtype, name, vmem_mib, cast=None):
    rows, d = u.shape
    ja, jb = col_a // tn, col_b // tn
    nj = width // tn
    assert ja * tn == col_a and jb * tn == col_b and nj * tn == width
    in_specs = [pl.BlockSpec((tm, d), lambda i, j: (i, 0)),
                pl.BlockSpec((d, tn), lambda i, j: (0, j + ja)),
                pl.BlockSpec((d, tn), lambda i, j: (0, j + jb)),
                pl.BlockSpec((1, tn), lambda i, j: (0, j + ja)),
                pl.BlockSpec((1, tn), lambda i, j: (0, j + jb))]
    args = [u, w, w, b, b]
    out_shape = jax.ShapeDtypeStruct((rows, width), out_dtype)
    out_specs = pl.BlockSpec((tm, tn), lambda i, j: (i, j))
    if cast is not None:
        spec, shape, steps = _side_cast_specs(cast[0], cast[1], nj)
        assert steps <= (rows // tm) * nj
        in_specs.append(spec)
        args.append(cast[0])
        out_shape, out_specs = (out_shape, shape), (out_specs, spec)
    return pl.pallas_call(
        functools.partial(_pair_kernel, combine=combine),
        out_shape=out_shape,
        grid=(rows // tm, nj),
        in_specs=in_specs,
        out_specs=out_specs,
        compiler_params=_params(vmem_mib, "arbitrary", "arbitrary"),
        name=name,
    )(*args)


CONV_HALO = 32
CONV_ROWS = 8
CONV_CHAINS = 2


def _conv_chunk(first, c_ref, cm_ref, w_ref, cb_ref, g_ref, b_ref, o_ref, ext_ref, y_ref):
    lane_tiles, rows = ext_ref.shape[0], c_ref.shape[0]
    lead = CONV_HALO - (CONV_WIDTH - 1)
    groups = rows // CONV_ROWS

    @pl.when(first)
    def _():
        for ct in range(lane_tiles):
            ext_ref[ct, 0:CONV_HALO - N_META, :] = jnp.zeros((CONV_HALO - N_META, V7X_LANES), F32)
            ext_ref[ct, CONV_HALO - N_META:CONV_HALO, :] = cm_ref[:, ct * V7X_LANES:(ct + 1) * V7X_LANES]

    @pl.when(jnp.logical_not(first))
    def _():
        for ct in range(lane_tiles):
            ext_ref[ct, 0:CONV_HALO, :] = ext_ref[ct, rows:rows + CONV_HALO, :]

    for ct in range(lane_tiles):
        ext_ref[ct, CONV_HALO:, :] = c_ref[:, ct * V7X_LANES:(ct + 1) * V7X_LANES]

    prev = [None] * CONV_CHAINS
    row_sum = [None] * groups
    n = 0
    for ct in range(lane_tiles):
        ls = slice(ct * V7X_LANES, (ct + 1) * V7X_LANES)
        for j in range(groups):
            acc = _after(jnp.broadcast_to(cb_ref[:, ls], (CONV_ROWS, V7X_LANES)), prev[n % CONV_CHAINS])
            for k in range(CONV_WIDTH):
                tap = jnp.broadcast_to(w_ref[k:k + 1, ls], (CONV_ROWS, V7X_LANES))
                r0 = j * CONV_ROWS + lead + k
                acc = acc + ext_ref[ct, r0:r0 + CONV_ROWS, :] * tap
            y_ref[j * CONV_ROWS:(j + 1) * CONV_ROWS, ls] = acc
            row_sum[j] = acc if row_sum[j] is None else row_sum[j] + acc
            prev[n % CONV_CHAINS] = acc
            n += 1

    inv_n = 1.0 / (lane_tiles * V7X_LANES)
    for j in range(groups):
        rs = slice(j * CONV_ROWS, (j + 1) * CONV_ROWS)
        mu = jnp.sum(row_sum[j], axis=1, keepdims=True) * inv_n
        parts = [None, None]
        for ct in range(lane_tiles):
            d = y_ref[rs, ct * V7X_LANES:(ct + 1) * V7X_LANES] - mu
            parts[ct % 2] = d * d if parts[ct % 2] is None else parts[ct % 2] + d * d
        var = jnp.sum(parts[0] + parts[1], axis=1, keepdims=True) * inv_n
        rstd = lax.rsqrt(var + EPS)
        for ct in range(lane_tiles):
            ls = slice(ct * V7X_LANES, (ct + 1) * V7X_LANES)
            yn = (y_ref[rs, ls] - mu) * rstd * g_ref[:, ls] + b_ref[:, ls]
            y_ref[rs, ls] = yn * jax.nn.sigmoid(yn)
    o_ref[...] = y_ref[...].astype(o_ref.dtype)


def _gates_conv_kernel(u_ref, w_ref, b_ref, c_ref, cm_ref, cw_ref, cb_ref, g_ref, bb_ref, cast_in_ref,
                       o_ref, co_ref, cast_out_ref, ext_ref, y_ref, *, seq_tiles):
    i, j = pl.program_id(0), pl.program_id(1)
    cast_out_ref[...] = cast_in_ref[...].astype(BF16)
    first = jnp.logical_and(i % seq_tiles == 0, j == 0)
    _conv_chunk(first, c_ref, cm_ref, cw_ref, cb_ref, g_ref, bb_ref, co_ref, ext_ref, y_ref)
    u = u_ref[...]
    for c0 in range(0, o_ref.shape[1], MXU_COLS):
        cs = slice(c0, c0 + MXU_COLS)
        z = jnp.dot(u, w_ref[:, cs].astype(BF16), preferred_element_type=F32) + b_ref[:, cs]
        o_ref[:, cs] = jax.nn.sigmoid(z).astype(o_ref.dtype)


def _gates_conv(u, w, b, c, c_meta, conv_w, conv_b, ln_g, ln_b, cast, *, col0, width, tm, tn, rows_per_seq):
    rows, d = u.shape
    channels = c.shape[1]
    j0, nj = col0 // tn, width // tn
    assert j0 * tn == col0 and nj * tn == width and tm % nj == 0
    chunk = tm // nj
    assert chunk % V7X_BF16_ROWS == 0 and chunk >= CONV_HALO
    small = lambda i, j: (0, 0)
    cast_spec, cast_shape, cast_steps = _side_cast_specs(cast[0], cast[1], nj)
    assert cast_steps <= (rows // tm) * nj
    return pl.pallas_call(
        functools.partial(_gates_conv_kernel, seq_tiles=rows_per_seq // tm),
        out_shape=(jax.ShapeDtypeStruct((rows, width), BF16), jax.ShapeDtypeStruct((rows, channels), BF16),
                   cast_shape),
        grid=(rows // tm, nj),
        in_specs=[pl.BlockSpec((tm, d), lambda i, j: (i, 0)),
                  pl.BlockSpec((d, tn), lambda i, j: (0, j + j0)),
                  pl.BlockSpec((1, tn), lambda i, j: (0, j + j0)),
                  pl.BlockSpec((chunk, channels), lambda i, j: (i * nj + j, 0)),
                  pl.BlockSpec((N_META, channels), small),
                  pl.BlockSpec((CONV_WIDTH, channels), small),
                  pl.BlockSpec((1, channels), small),
                  pl.BlockSpec((1, channels), small),
                  pl.BlockSpec((1, channels), small),
                  cast_spec],
        out_specs=(pl.BlockSpec((tm, tn), lambda i, j: (i, j)),
                   pl.BlockSpec((chunk, channels), lambda i, j: (i * nj + j, 0)),
                   cast_spec),
        scratch_shapes=[pltpu.VMEM((channels // V7X_LANES, chunk + CONV_HALO, V7X_LANES), F32),
                        pltpu.VMEM((chunk, channels), F32)],
        compiler_params=_params(52, "arbitrary", "arbitrary"),
        name="gates_conv",
    )(u, w, b, c, c_meta, conv_w, conv_b.reshape(1, channels), ln_g.reshape(1, channels),
      ln_b.reshape(1, channels), cast[0])


ATT_KEYS = 3 * WINDOW
PAIRS = GROUP // 2


def _attn_kernel(sink_ref, q_ref, kc_ref, kp_ref, vc_ref, vp_ref, kvm_ref, o_ref):
    first_block = pl.program_id(1) == 0
    rows = PAIRS * WINDOW
    r = lax.broadcasted_iota(jnp.int32, (rows, ATT_KEYS), 0) & (WINDOW - 1)
    c = lax.broadcasted_iota(jnp.int32, (rows, ATT_KEYS), 1)
    c_min = jnp.where(first_block, WINDOW, 0)
    band = (c > r) & (c <= r + WINDOW) & (c >= c_min)
    meta = (c >= 2 * WINDOW) & (c < 2 * WINDOW + N_META)
    bias = jnp.where(band | meta, 0.0, -jnp.inf).astype(F32)

    row_pair = lax.shift_right_logical(lax.broadcasted_iota(jnp.int32, (rows, 1), 0), WINDOW.bit_length() - 1)
    lane = lax.broadcasted_iota(jnp.int32, (ATT_KEYS, V7X_LANES), 1)
    low = lane < HEAD_DIM
    zero_pad = jnp.zeros((WINDOW - N_META, V7X_LANES), F32)

    units = []
    for g in range(N_KV_HEADS):
        tile, upper = g // 2, g % 2
        sl = slice(tile * V7X_LANES, (tile + 1) * V7X_LANES)

        def halves(prev_ref, cur_ref, meta_cols):
            x = jnp.concatenate([prev_ref[:, sl].astype(F32), cur_ref[:, sl].astype(F32),
                                 kvm_ref[:, meta_cols].astype(F32), zero_pad], axis=0)
            swapped = pltpu.roll(x, HEAD_DIM, 1)
            lo_src, hi_src = (swapped, x) if upper else (x, swapped)
            return (jnp.where(low, lo_src, 0.0).astype(BF16), jnp.where(low, 0.0, hi_src).astype(BF16))

        k_even, k_odd = halves(kp_ref, kc_ref, slice(tile * V7X_LANES, (tile + 1) * V7X_LANES))
        v_even, v_odd = halves(vp_ref, vc_ref, slice(KV_DIM + tile * V7X_LANES, KV_DIM + (tile + 1) * V7X_LANES))

        qp = jnp.concatenate(
            [q_ref[:, g * GROUP * HEAD_DIM + m * V7X_LANES: g * GROUP * HEAD_DIM + (m + 1) * V7X_LANES]
             for m in range(PAIRS)], axis=0)

        for parity, (kz, vz) in enumerate(((k_even, v_even), (k_odd, v_odd))):
            sink = jnp.zeros((rows, 1), F32)
            for m in range(PAIRS):
                sink = jnp.where(row_pair == m, sink_ref[g * GROUP + 2 * m + parity], sink)
            units.append((qp, kz, vz, sink))

    scores = [lax.dot_general(qp, kz, (((1,), (1,)), ((), ())), preferred_element_type=F32) + bias
              for qp, kz, _, _ in units]
    maxes = [jnp.maximum(jnp.max(s, axis=1, keepdims=True), unit[3]) for s, unit in zip(scores, units)]
    probs = [jnp.exp(s - mx) for s, mx in zip(scores, maxes)]
    denoms = [jnp.sum(p, axis=1, keepdims=True) + jnp.exp(unit[3] - mx)
              for p, mx, unit in zip(probs, maxes, units)]
    outs = [jnp.dot(p.astype(BF16), unit[2], preferred_element_type=F32) * (1.0 / den)
            for p, den, unit in zip(probs, denoms, units)]
    for g in range(N_KV_HEADS):
        out = outs[2 * g] + outs[2 * g + 1]
        for m in range(PAIRS):
            col = g * GROUP * HEAD_DIM + m * V7X_LANES
            o_ref[:, col:col + V7X_LANES] = out[m * WINDOW:(m + 1) * WINDOW, :].astype(o_ref.dtype)


def _attention(qkv, kv_meta, sinks, *, batch, seq):
    nb = seq // WINDOW
    kcol, vcol = Q_DIM // KV_DIM, Q_DIM // KV_DIM + 1

    def cur(col):
        return lambda b, j: (b * nb + j, col)

    def prev(col):
        return lambda b, j: (jnp.maximum(b * nb + j - 1, 0), col)

    return pl.pallas_call(
        _attn_kernel,
        out_shape=jax.ShapeDtypeStruct((batch * seq, Q_DIM), BF16),
        grid=(batch, nb),
        in_specs=[pl.BlockSpec(memory_space=pltpu.SMEM),
                  pl.BlockSpec((WINDOW, Q_DIM), lambda b, j: (b * nb + j, 0)),
                  pl.BlockSpec((WINDOW, KV_DIM), cur(kcol)),
                  pl.BlockSpec((WINDOW, KV_DIM), prev(kcol)),
                  pl.BlockSpec((WINDOW, KV_DIM), cur(vcol)),
                  pl.BlockSpec((WINDOW, KV_DIM), prev(vcol)),
                  pl.BlockSpec((N_META, 2 * KV_DIM), lambda b, j: (0, 0))],
        out_specs=pl.BlockSpec((WINDOW, Q_DIM), lambda b, j: (b * nb + j, 0)),
        compiler_params=_params(32, "arbitrary", "arbitrary"),
        name="swa_attention",
    )(sinks, qkv, qkv, qkv, qkv, qkv, kv_meta)


def _merge_kernel(a_ref, c_ref, wa_ref, wc_ref, bc_ref, ga_ref, gb_ref, o_ref):
    a = jnp.dot(a_ref[...], wa_ref[...].astype(BF16), preferred_element_type=F32)
    b = jnp.dot(c_ref[...], wc_ref[...].astype(BF16), preferred_element_type=F32) + bc_ref[...]
    o_ref[...] = (ga_ref[...].astype(F32) * a + gb_ref[...].astype(F32) * b).astype(o_ref.dtype)


def _merge(attn, conv, w_ao, w_co, b_co, gates, *, tm, tn):
    rows, k = attn.shape
    d = w_ao.shape[1]
    nj = d // tn
    return pl.pallas_call(
        _merge_kernel,
        out_shape=jax.ShapeDtypeStruct((rows, d), BF16),
        grid=(rows // tm, nj),
        in_specs=[pl.BlockSpec((tm, k), lambda i, j: (i, 0)),
                  pl.BlockSpec((tm, k), lambda i, j: (i, 0)),
                  pl.BlockSpec((k, tn), lambda i, j: (0, j)),
                  pl.BlockSpec((k, tn), lambda i, j: (0, j)),
                  pl.BlockSpec((1, tn), lambda i, j: (0, j)),
                  pl.BlockSpec((tm, tn), lambda i, j: (i, j)),
                  pl.BlockSpec((tm, tn), lambda i, j: (i, j + nj))],
        out_specs=pl.BlockSpec((tm, tn), lambda i, j: (i, j)),
        compiler_params=_params(48, "arbitrary", "arbitrary"),
        name="gated_merge",
    )(attn, conv, w_ao, w_co, b_co.reshape(1, d), gates, gates)


def _residual_kernel(x_ref, w_ref, r_ref, o_ref):
    x = x_ref[...]
    for c0 in range(0, o_ref.shape[1], MXU_COLS):
        cs = slice(c0, c0 + MXU_COLS)
        o_ref[:, cs] = r_ref[:, cs] + jnp.dot(x, w_ref[:, cs].astype(BF16), preferred_element_type=F32)


def _residual_matmul(x, w, resid, *, tm, tn, vmem_mib, single_buffer_x=False):
    rows, k = x.shape
    n = w.shape[1]
    x_spec = (pl.BlockSpec((tm, k), lambda i, j: (i, 0), pipeline_mode=pl.Buffered(1))
              if single_buffer_x else pl.BlockSpec((tm, k), lambda i, j: (i, 0)))
    return pl.pallas_call(
        _residual_kernel,
        out_shape=jax.ShapeDtypeStruct((rows, n), F32),
        grid=(rows // tm, n // tn),
        in_specs=[x_spec,
                  pl.BlockSpec((k, tn), lambda i, j: (0, j)),
                  pl.BlockSpec((tm, tn), lambda i, j: (i, j))],
        out_specs=pl.BlockSpec((tm, tn), lambda i, j: (i, j)),
        compiler_params=_params(vmem_mib, "arbitrary", "arbitrary"),
        name="residual_matmul",
    )(x, w, resid)


FFN_DOT_ROWS = 1024


def _ffn_up_kernel(h_ref, g_ref, wa_ref, wb_ref, o_ref, u0_ref, u1_ref, *, n_tiles, chunks):
    i, j = pl.program_id(0), pl.program_id(1)
    chunk = h_ref.shape[0]
    slots = (u0_ref, u1_ref)

    def norm(slot):
        base = pl.multiple_of(jnp.minimum(j, chunks - 1) * chunk, chunk)

        def store(r0, rows):
            slots[slot][pl.ds(pl.multiple_of(base + r0, V7X_BF16_ROWS), V7X_BF16_ROWS), :] = rows

        _rmsnorm_chunk(h_ref, g_ref, store)

    def matmul(slot):
        for r0 in range(0, o_ref.shape[0], FFN_DOT_ROWS):
            u = slots[slot][r0:r0 + FFN_DOT_ROWS, :]
            a = jnp.dot(u, wa_ref[...].astype(BF16), preferred_element_type=F32)
            b = jnp.dot(u, wb_ref[...].astype(BF16), preferred_element_type=F32)
            o_ref[r0:r0 + FFN_DOT_ROWS, :] = _swiglu(a, b).astype(o_ref.dtype)

    _next_tile_norm_body(i, j, n_tiles, chunks, norm, matmul)


def _ffn_up(h, g, w, *, ffn, tm, tn, chunks):
    rows, d = h.shape
    n_tiles, nj, chunk = rows // tm, ffn // tn, tm // chunks
    assert nj * tn == ffn and nj >= chunks
    chunk_map, w_col, out_row = _next_tile_norm_maps(n_tiles, chunks)
    return pl.pallas_call(
        functools.partial(_ffn_up_kernel, n_tiles=n_tiles, chunks=chunks),
        out_shape=jax.ShapeDtypeStruct((rows, ffn), BF16),
        grid=(n_tiles + 1, nj),
        in_specs=[pl.BlockSpec((chunk, d), chunk_map),
                  pl.BlockSpec((1, d), lambda i, j: (0, 0)),
                  pl.BlockSpec((d, tn), lambda i, j: (0, w_col(i, j))),
                  pl.BlockSpec((d, tn), lambda i, j: (0, w_col(i, j) + nj))],
        out_specs=pl.BlockSpec((tm, tn), lambda i, j: (out_row(i), w_col(i, j))),
        scratch_shapes=[pltpu.VMEM((tm, d), BF16), pltpu.VMEM((tm, d), BF16)],
        compiler_params=_params(56, "arbitrary", "arbitrary"),
        name="ffn_gate_up",
    )(h, g.reshape(1, d), w, w)


def _rope_tables(first_pos, length):
    pos = jnp.arange(first_pos, first_pos + length, dtype=F32)
    inv_freq = ROPE_THETA ** (-jnp.arange(0, HEAD_DIM, 2, dtype=F32) / HEAD_DIM)
    ang = pos[:, None] * inv_freq[None, :]
    cos, sin = jnp.cos(ang), jnp.sin(ang)
    reps = V7X_LANES // HEAD_DIM
    return (jnp.tile(jnp.concatenate([cos, cos], axis=1), (1, reps)),
            jnp.tile(jnp.concatenate([-sin, sin], axis=1), (1, reps)))


def kernel(x, meta_tokens, mix_norm_g, w_in, b_in, attn_sinks, conv_w, conv_b, conv_ln_g, conv_ln_b,
           w_attn_o, w_conv_o, b_conv_o, w_out, ffn_norm_g, w_gate_up, w_down, final_norm_g):
    batch, seq, d = x.shape
    depth = w_in.shape[0]
    assert depth == 1, "one layer: the meta rows' outputs are never needed"
    conv_dim = conv_w.shape[-1]
    ffn = w_down.shape[1]
    in_dim = w_in.shape[-1]
    rows = batch * seq
    col_conv = Q_DIM + 2 * KV_DIM
    col_gate = col_conv + 2 * conv_dim
    assert in_dim == col_gate + 2 * d

    w_in2 = w_in.reshape(d, in_dim)
    b_in2 = b_in.reshape(1, in_dim)
    h0 = x.reshape(rows, d)
    tm = 1024
    qkv_tn = 512
    qkv_nj = (Q_DIM + 2 * KV_DIM) // qkv_tn

    cos, sin_signed = _rope_tables(N_META, seq)
    cos_m, sin_m = _rope_tables(0, N_META)

    qkv, u = _qkv_norm_proj(h0, mix_norm_g[0], w_in2, b_in2, cos, sin_signed, tm=tm, tn=qkv_tn,
                            nj=qkv_nj, chunks=qkv_nj - 1, rows_per_seq=seq)
    u_meta = _rmsnorm(meta_tokens, mix_norm_g[0], BF16, tm=N_META)
    kv_meta = _qkv_proj(u_meta, w_in2, b_in2, cos_m, sin_m, tm=N_META, tn=qkv_tn, j0=Q_DIM // qkv_tn, nj=1)
    glu = functools.partial(_pair_proj, w=w_in2, b=b_in2, col_a=col_conv, col_b=col_conv + conv_dim,
                            width=conv_dim, tn=256, combine=_glu, out_dtype=F32, name="conv_glu_proj")
    c, w_down_bf16 = glu(u, tm=tm, vmem_mib=52, cast=(w_down[0], 128))
    c_meta = glu(u_meta, tm=N_META, vmem_mib=32)

    gates, conv, w_gate_up_bf16 = _gates_conv(
        u, w_in2, b_in2, c, c_meta, conv_w.reshape(CONV_WIDTH, conv_dim), conv_b[0], conv_ln_g[0],
        conv_ln_b[0], (w_gate_up[0], V7X_BF16_ROWS), col0=col_gate, width=2 * d, tm=tm, tn=512,
        rows_per_seq=seq)
    attn = _attention(qkv, kv_meta, attn_sinks[0], batch=batch, seq=seq)

    merged = _merge(attn, conv, w_attn_o[0], w_conv_o[0], b_conv_o[0], gates, tm=tm, tn=512)
    h1 = _residual_matmul(merged, w_out[0], h0, tm=tm, tn=512, vmem_mib=48)

    act = _ffn_up(h1, ffn_norm_g[0], w_gate_up_bf16, ffn=ffn, tm=2 * tm, tn=256, chunks=16)
    h2 = _residual_matmul(act, w_down_bf16, h1, tm=tm, tn=512, vmem_mib=58, single_buffer_x=True)

    y = _rmsnorm(h2, final_norm_g, x.dtype, tm=256)
    return y.reshape(batch, seq, d)
```

```python
import functools

import jax
import jax.numpy as jnp
from jax import lax
from jax.experimental import pallas as pl
from jax.experimental.pallas import tpu as pltpu

N_META = 16
HEAD_DIM = 64
N_Q_HEADS = 32
N_KV_HEADS = 4
GROUP = N_Q_HEADS // N_KV_HEADS
WINDOW = 128
Q_DIM = N_Q_HEADS * HEAD_DIM
KV_DIM = N_KV_HEADS * HEAD_DIM
CONV_WIDTH = 31
ROPE_THETA = 10000.0
EPS = 1e-6

V7X_LANES = 128
V7X_BF16_ROWS = 16
V7X_VMEM_BYTES = 64 * 1024 * 1024
MXU_COLS = 256
DOT_ROWS = 1024
MIB = 1024 * 1024

F32 = jnp.float32
BF16 = jnp.bfloat16


def _params(vmem_mib, *sem):
    assert vmem_mib * MIB < V7X_VMEM_BYTES
    return pltpu.CompilerParams(dimension_semantics=sem, vmem_limit_bytes=vmem_mib * MIB)


def _after(x, dep):
    if dep is None:
        return x
    zero = lax.shift_right_logical(
        lax.shift_right_logical(lax.bitcast_convert_type(dep, jnp.uint32), jnp.uint32(16)), jnp.uint32(16))
    return lax.bitcast_convert_type(lax.bitcast_convert_type(x, jnp.uint32) | zero, x.dtype)


def _rmsnorm_kernel(x_ref, g_ref, o_ref):
    x = x_ref[...]
    ms = jnp.mean(x * x, axis=-1, keepdims=True)
    o_ref[...] = (x * lax.rsqrt(ms + EPS) * g_ref[...]).astype(o_ref.dtype)


def _rmsnorm(x, g, out_dtype, tm):
    rows, d = x.shape
    return pl.pallas_call(
        _rmsnorm_kernel,
        out_shape=jax.ShapeDtypeStruct((rows, d), out_dtype),
        grid=(rows // tm,),
        in_specs=[pl.BlockSpec((tm, d), lambda i: (i, 0)),
                  pl.BlockSpec((1, d), lambda i: (0, 0))],
        out_specs=pl.BlockSpec((tm, d), lambda i: (i, 0)),
        compiler_params=_params(32, "arbitrary"),
        name="rmsnorm",
    )(x, g.reshape(1, d))


def _rmsnorm_chunk(x_ref, g_ref, store):
    dep = None
    for r0 in range(0, x_ref.shape[0], V7X_BF16_ROWS):
        x = x_ref[r0:r0 + V7X_BF16_ROWS, :]
        x = jnp.concatenate([_after(x[:, :V7X_LANES], dep), x[:, V7X_LANES:]], axis=1)
        ms = jnp.mean(x * x, axis=-1, keepdims=True)
        y = x * lax.rsqrt(ms + EPS) * g_ref[...]
        store(r0, y.astype(BF16))
        dep = y[:, :V7X_LANES]


def _next_tile_norm_maps(n_tiles, chunks):
    last = n_tiles * chunks - 1

    def chunk_map(i, j):
        return (jnp.where(i >= n_tiles, last, i * chunks + jnp.minimum(j, chunks - 1)), 0)

    def w_col(i, j):
        return jnp.where(i == 0, 0, j)

    def out_row(i):
        return jnp.maximum(i - 1, 0)

    return chunk_map, w_col, out_row


def _next_tile_norm_body(i, j, n_tiles, chunks, norm, matmul):
    do_norm = jnp.logical_and(j < chunks, i < n_tiles)

    @pl.when(jnp.logical_and(i == 0, do_norm))
    def _():
        norm(0)

    for slot in (0, 1):
        mine = jnp.logical_and(i > 0, i % 2 == slot)

        @pl.when(jnp.logical_and(mine, do_norm))
        def _():
            norm(slot)
            matmul(1 - slot)

        @pl.when(jnp.logical_and(mine, jnp.logical_not(do_norm)))
        def _():
            matmul(1 - slot)


def _rope(z, cos, sin_signed):
    lane = lax.broadcasted_iota(jnp.int32, (z.shape[0], V7X_LANES), 1)
    first_half = (lane & (HEAD_DIM - 1)) < HEAD_DIM // 2
    out = []
    for t in range(z.shape[1] // V7X_LANES):
        zt = z[:, t * V7X_LANES:(t + 1) * V7X_LANES]
        partner = jnp.where(first_half, pltpu.roll(zt, V7X_LANES - HEAD_DIM // 2, 1),
                            pltpu.roll(zt, HEAD_DIM // 2, 1))
        out.append(zt * cos + partner * sin_signed)
    return jnp.concatenate(out, axis=1)


def _qkv_epilogue(z, j, tn, cos, sin_signed):
    roped = _rope(z, cos, sin_signed)
    lane = lax.broadcasted_iota(jnp.int32, z.shape, 1)
    is_q = j < Q_DIM // tn
    use_rope = jnp.logical_or(is_q, lane < KV_DIM)
    scale = jnp.where(is_q, HEAD_DIM ** -0.5, 1.0).astype(F32)
    return (jnp.where(use_rope, roped, z) * scale).astype(BF16)


def _qkv_kernel(u_ref, w_ref, b_ref, cos_ref, sin_ref, o_ref, *, tn, j0):
    z = jnp.dot(u_ref[...], w_ref[...].astype(BF16), preferred_element_type=F32) + b_ref[...]
    o_ref[...] = _qkv_epilogue(z, pl.program_id(1) + j0, tn, cos_ref[...], sin_ref[...])


def _qkv_proj(u, w_in, b_in, cos, sin_signed, *, tm, tn, j0, nj):
    rows, d = u.shape
    assert rows == tm
    return pl.pallas_call(
        functools.partial(_qkv_kernel, tn=tn, j0=j0),
        out_shape=jax.ShapeDtypeStruct((rows, nj * tn), BF16),
        grid=(1, nj),
        in_specs=[pl.BlockSpec((tm, d), lambda i, j: (0, 0)),
                  pl.BlockSpec((d, tn), lambda i, j: (0, j + j0)),
                  pl.BlockSpec((1, tn), lambda i, j: (0, j + j0)),
                  pl.BlockSpec((tm, V7X_LANES), lambda i, j: (0, 0)),
                  pl.BlockSpec((tm, V7X_LANES), lambda i, j: (0, 0))],
        out_specs=pl.BlockSpec((tm, tn), lambda i, j: (0, j)),
        compiler_params=_params(32, "arbitrary", "arbitrary"),
        name="qkv_proj_meta",
    )(u, w_in, b_in, cos, sin_signed)


def _qkv_norm_kernel(x_ref, g_ref, w_ref, b_ref, cos_ref, sin_ref, o_ref, uo_ref, u0_ref, u1_ref,
                     *, tn, n_tiles, chunks):
    i, j = pl.program_id(0), pl.program_id(1)
    chunk = x_ref.shape[0]
    slots = (u0_ref, u1_ref)

    def norm(slot):
        base = pl.multiple_of(jnp.minimum(j, chunks - 1) * chunk, chunk)

        def store(r0, rows):
            slots[slot][pl.ds(pl.multiple_of(base + r0, V7X_BF16_ROWS), V7X_BF16_ROWS), :] = rows
            uo_ref[r0:r0 + V7X_BF16_ROWS, :] = rows

        _rmsnorm_chunk(x_ref, g_ref, store)

    def matmul(slot):
        z = jnp.dot(slots[slot][...], w_ref[...].astype(BF16), preferred_element_type=F32) + b_ref[...]
        o_ref[...] = _qkv_epilogue(z, j, tn, cos_ref[...], sin_ref[...])

    _next_tile_norm_body(i, j, n_tiles, chunks, norm, matmul)


def _qkv_norm_proj(x, g, w_in, b_in, cos, sin_signed, *, tm, tn, nj, chunks, rows_per_seq):
    rows, d = x.shape
    n_tiles, seq_tiles, chunk = rows // tm, rows_per_seq // tm, tm // chunks
    chunk_map, w_col, out_row = _next_tile_norm_maps(n_tiles, chunks)
    return pl.pallas_call(
        functools.partial(_qkv_norm_kernel, tn=tn, n_tiles=n_tiles, chunks=chunks),
        out_shape=(jax.ShapeDtypeStruct((rows, nj * tn), BF16), jax.ShapeDtypeStruct((rows, d), BF16)),
        grid=(n_tiles + 1, nj),
        in_specs=[pl.BlockSpec((chunk, d), chunk_map),
                  pl.BlockSpec((1, d), lambda i, j: (0, 0)),
                  pl.BlockSpec((d, tn), lambda i, j: (0, w_col(i, j))),
                  pl.BlockSpec((1, tn), lambda i, j: (0, w_col(i, j))),
                  pl.BlockSpec((tm, V7X_LANES), lambda i, j: (out_row(i) % seq_tiles, 0)),
                  pl.BlockSpec((tm, V7X_LANES), lambda i, j: (out_row(i) % seq_tiles, 0))],
        out_specs=(pl.BlockSpec((tm, tn), lambda i, j: (out_row(i), w_col(i, j))),
                   pl.BlockSpec((chunk, d), chunk_map)),
        scratch_shapes=[pltpu.VMEM((tm, d), BF16), pltpu.VMEM((tm, d), BF16)],
        compiler_params=_params(58, "arbitrary", "arbitrary"),
        name="qkv_proj",
    )(x, g.reshape(1, d), w_in, b_in, cos, sin_signed)


def _glu(a, g):
    return a * jax.nn.sigmoid(g)


def _swiglu(g, up):
    return g * jax.nn.sigmoid(g) * up


def _side_cast_specs(weight, block_rows, nj, first_step=0):
    w_rows, w_cols = weight.shape
    assert w_rows % block_rows == 0 and block_rows % V7X_BF16_ROWS == 0
    last = w_rows // block_rows - 1
    spec = pl.BlockSpec((block_rows, w_cols),
                        lambda i, j: (jnp.clip(i * nj + j - first_step, 0, last), 0))
    return spec, jax.ShapeDtypeStruct((w_rows, w_cols), BF16), first_step + last + 1


def _pair_kernel(u_ref, wa_ref, wb_ref, ba_ref, bb_ref, *rest, combine):
    if len(rest) == 3:
        cast_in_ref, o_ref, cast_out_ref = rest
        cast_out_ref[...] = cast_in_ref[...].astype(BF16)
    else:
        (o_ref,) = rest
    a = jnp.dot(u_ref[...], wa_ref[...].astype(BF16), preferred_element_type=F32) + ba_ref[...]
    b = jnp.dot(u_ref[...], wb_ref[...].astype(BF16), preferred_element_type=F32) + bb_ref[...]
    o_ref[...] = combine(a, b).astype(o_ref.dtype)


def _pair_proj(u, w, b, *, col_a, col_b, width, tm, tn, combine, out_dtype, name, vmem_mib, cast=None):
    rows, d = u.shape
    ja, jb = col_a // tn, col_b // tn
    nj = width // tn
    assert ja * tn == col_a and jb * tn == col_b and nj * tn == width
    in_specs = [pl.BlockSpec((tm, d), lambda i, j: (i, 0)),
                pl.BlockSpec((d, tn), lambda i, j: (0, j + ja)),
                pl.BlockSpec((d, tn), lambda i, j: (0, j + jb)),
                pl.BlockSpec((1, tn), lambda i, j: (0, j + ja)),
                pl.BlockSpec((1, tn), lambda i, j: (0, j + jb))]
    args = [u, w, w, b, b]
    out_shape = jax.ShapeDtypeStruct((rows, width), out_dtype)
    out_specs = pl.BlockSpec((tm, tn), lambda i, j: (i, j))
    if cast is not None:
        spec, shape, steps = _side_cast_specs(cast[0], cast[1], nj)
        assert steps <= (rows // tm) * nj
        in_specs.append(spec)
        args.append(cast[0])
        out_shape, out_specs = (out_shape, shape), (out_specs, spec)
    return pl.pallas_call(
        functools.partial(_pair_kernel, combine=combine),
        out_shape=out_shape,
        grid=(rows // tm, nj),
        in_specs=in_specs,
        out_specs=out_specs,
        compiler_params=_params(vmem_mib, "arbitrary", "arbitrary"),
        name=name,
    )(*args)


CONV_HALO = 32
CONV_ROWS = 8
CONV_CHAINS = 2


def _conv_chunk(first, c_ref, cm_ref, w_ref, cb_ref, g_ref, b_ref, o_ref, ext_ref, y_ref):
    lane_tiles, rows = ext_ref.shape[0], c_ref.shape[0]
    lead = CONV_HALO - (CONV_WIDTH - 1)
    groups = rows // CONV_ROWS

    @pl.when(first)
    def _():
        for ct in range(lane_tiles):
            ext_ref[ct, 0:CONV_HALO - N_META, :] = jnp.zeros((CONV_HALO - N_META, V7X_LANES), F32)
            ext_ref[ct, CONV_HALO - N_META:CONV_HALO, :] = cm_ref[:, ct * V7X_LANES:(ct + 1) * V7X_LANES]

    @pl.when(jnp.logical_not(first))
    def _():
        for ct in range(lane_tiles):
            ext_ref[ct, 0:CONV_HALO, :] = ext_ref[ct, rows:rows + CONV_HALO, :]

    for ct in range(lane_tiles):
        ext_ref[ct, CONV_HALO:, :] = c_ref[:, ct * V7X_LANES:(ct + 1) * V7X_LANES]

    prev = [None] * CONV_CHAINS
    row_sum = [None] * groups
    n = 0
    for ct in range(lane_tiles):
        ls = slice(ct * V7X_LANES, (ct + 1) * V7X_LANES)
        for j in range(groups):
            acc = _after(jnp.broadcast_to(cb_ref[:, ls], (CONV_ROWS, V7X_LANES)), prev[n % CONV_CHAINS])
            for k in range(CONV_WIDTH):
                tap = jnp.broadcast_to(w_ref[k:k + 1, ls], (CONV_ROWS, V7X_LANES))
                r0 = j * CONV_ROWS + lead + k
                acc = acc + ext_ref[ct, r0:r0 + CONV_ROWS, :] * tap
            y_ref[j * CONV_ROWS:(j + 1) * CONV_ROWS, ls] = acc
            row_sum[j] = acc if row_sum[j] is None else row_sum[j] + acc
            prev[n % CONV_CHAINS] = acc
            n += 1

    inv_n = 1.0 / (lane_tiles * V7X_LANES)
    for j in range(groups):
        rs = slice(j * CONV_ROWS, (j + 1) * CONV_ROWS)
        mu = jnp.sum(row_sum[j], axis=1, keepdims=True) * inv_n
        parts = [None, None]
        for ct in range(lane_tiles):
            d = y_ref[rs, ct * V7X_LANES:(ct + 1) * V7X_LANES] - mu
            parts[ct % 2] = d * d if parts[ct % 2] is None else parts[ct % 2] + d * d
        var = jnp.sum(parts[0] + parts[1], axis=1, keepdims=True) * inv_n
        rstd = lax.rsqrt(var + EPS)
        for ct in range(lane_tiles):
            ls = slice(ct * V7X_LANES, (ct + 1) * V7X_LANES)
            yn = (y_ref[rs, ls] - mu) * rstd * g_ref[:, ls] + b_ref[:, ls]
            y_ref[rs, ls] = yn * jax.nn.sigmoid(yn)
    o_ref[...] = y_ref[...].astype(o_ref.dtype)


def _gates_conv_kernel(u_ref, w_ref, b_ref, c_ref, cm_ref, cw_ref, cb_ref, g_ref, bb_ref, cast_in_ref,
                       o_ref, co_ref, cast_out_ref, ext_ref, y_ref, *, seq_tiles):
    i, j = pl.program_id(0), pl.program_id(1)
    cast_out_ref[...] = cast_in_ref[...].astype(BF16)
    first = jnp.logical_and(i % seq_tiles == 0, j == 0)
    _conv_chunk(first, c_ref, cm_ref, cw_ref, cb_ref, g_ref, bb_ref, co_ref, ext_ref, y_ref)
    for r0 in range(0, o_ref.shape[0], DOT_ROWS):
        rs = slice(r0, r0 + DOT_ROWS)
        for c0 in range(0, o_ref.shape[1], MXU_COLS):
            cs = slice(c0, c0 + MXU_COLS)
            z = jnp.dot(u_ref[rs, :], w_ref[:, cs].astype(BF16), preferred_element_type=F32) + b_ref[:, cs]
            o_ref[rs, cs] = jax.nn.sigmoid(z).astype(o_ref.dtype)


def _gates_conv(u, w, b, c, c_meta, conv_w, conv_b, ln_g, ln_b, cast, *, col0, width, tm, tn, rows_per_seq):
    rows, d = u.shape
    channels = c.shape[1]
    j0, nj = col0 // tn, width // tn
    assert j0 * tn == col0 and nj * tn == width and tm % nj == 0
    chunk = tm // nj
    assert chunk % V7X_BF16_ROWS == 0 and chunk >= CONV_HALO
    small = lambda i, j: (0, 0)
    cast_spec, cast_shape, cast_steps = _side_cast_specs(cast[0], cast[1], nj)
    assert cast_steps <= (rows // tm) * nj
    return pl.pallas_call(
        functools.partial(_gates_conv_kernel, seq_tiles=rows_per_seq // tm),
        out_shape=(jax.ShapeDtypeStruct((rows, width), BF16), jax.ShapeDtypeStruct((rows, channels), BF16),
                   cast_shape),
        grid=(rows // tm, nj),
        in_specs=[pl.BlockSpec((tm, d), lambda i, j: (i, 0)),
                  pl.BlockSpec((d, tn), lambda i, j: (0, j + j0)),
                  pl.BlockSpec((1, tn), lambda i, j: (0, j + j0)),
                  pl.BlockSpec((chunk, channels), lambda i, j: (i * nj + j, 0)),
                  pl.BlockSpec((N_META, channels), small),
                  pl.BlockSpec((CONV_WIDTH, channels), small),
                  pl.BlockSpec((1, channels), small),
                  pl.BlockSpec((1, channels), small),
                  pl.BlockSpec((1, channels), small),
                  cast_spec],
        out_specs=(pl.BlockSpec((tm, tn), lambda i, j: (i, j)),
                   pl.BlockSpec((chunk, channels), lambda i, j: (i * nj + j, 0)),
                   cast_spec),
        scratch_shapes=[pltpu.VMEM((channels // V7X_LANES, chunk + CONV_HALO, V7X_LANES), F32),
                        pltpu.VMEM((chunk, channels), F32)],
        compiler_params=_params(52, "arbitrary", "arbitrary"),
        name="gates_conv",
    )(u, w, b, c, c_meta, conv_w, conv_b.reshape(1, channels), ln_g.reshape(1, channels),
      ln_b.reshape(1, channels), cast[0])


ATT_KEYS = 3 * WINDOW
PAIRS = GROUP // 2


def _attn_kernel(sink_ref, q_ref, kc_ref, kp_ref, vc_ref, vp_ref, kvm_ref, *rest, cast_steps):
    n_cast = len(cast_steps)
    o_ref = rest[n_cast]
    step = pl.program_id(0) * pl.num_programs(1) + pl.program_id(1)
    for cast_in_ref, cast_out_ref, (lo, hi) in zip(rest[:n_cast], rest[n_cast + 1:], cast_steps):
        @pl.when(jnp.logical_and(step >= lo, step < hi))
        def _():
            cast_out_ref[...] = cast_in_ref[...].astype(BF16)
    first_block = pl.program_id(1) == 0
    rows = PAIRS * WINDOW
    r = lax.broadcasted_iota(jnp.int32, (rows, ATT_KEYS), 0) & (WINDOW - 1)
    c = lax.broadcasted_iota(jnp.int32, (rows, ATT_KEYS), 1)
    c_min = jnp.where(first_block, WINDOW, 0)
    band = (c > r) & (c <= r + WINDOW) & (c >= c_min)
    meta = (c >= 2 * WINDOW) & (c < 2 * WINDOW + N_META)
    bias = jnp.where(band | meta, 0.0, -jnp.inf).astype(F32)

    row_pair = lax.shift_right_logical(lax.broadcasted_iota(jnp.int32, (rows, 1), 0), WINDOW.bit_length() - 1)
    lane = lax.broadcasted_iota(jnp.int32, (ATT_KEYS, V7X_LANES), 1)
    low = lane < HEAD_DIM
    zero_pad = jnp.zeros((WINDOW - N_META, V7X_LANES), F32)

    units = []
    for g in range(N_KV_HEADS):
        tile, upper = g // 2, g % 2
        sl = slice(tile * V7X_LANES, (tile + 1) * V7X_LANES)

        def halves(prev_ref, cur_ref, meta_cols):
            x = jnp.concatenate([prev_ref[:, sl].astype(F32), cur_ref[:, sl].astype(F32),
                                 kvm_ref[:, meta_cols].astype(F32), zero_pad], axis=0)
            swapped = pltpu.roll(x, HEAD_DIM, 1)
            lo_src, hi_src = (swapped, x) if upper else (x, swapped)
            return (jnp.where(low, lo_src, 0.0).astype(BF16), jnp.where(low, 0.0, hi_src).astype(BF16))

        k_even, k_odd = halves(kp_ref, kc_ref, slice(tile * V7X_LANES, (tile + 1) * V7X_LANES))
        v_even, v_odd = halves(vp_ref, vc_ref, slice(KV_DIM + tile * V7X_LANES, KV_DIM + (tile + 1) * V7X_LANES))

        qp = jnp.concatenate(
            [q_ref[:, g * GROUP * HEAD_DIM + m * V7X_LANES: g * GROUP * HEAD_DIM + (m + 1) * V7X_LANES]
             for m in range(PAIRS)], axis=0)

        for parity, (kz, vz) in enumerate(((k_even, v_even), (k_odd, v_odd))):
            sink = jnp.zeros((rows, 1), F32)
            for m in range(PAIRS):
                sink = jnp.where(row_pair == m, sink_ref[g * GROUP + 2 * m + parity], sink)
            units.append((qp, kz, vz, sink))

    scores = [lax.dot_general(qp, kz, (((1,), (1,)), ((), ())), preferred_element_type=F32) + bias
              for qp, kz, _, _ in units]
    maxes = [jnp.maximum(jnp.max(s, axis=1, keepdims=True), unit[3]) for s, unit in zip(scores, units)]
    probs = [jnp.exp(s - mx) for s, mx in zip(scores, maxes)]
    denoms = [jnp.sum(p, axis=1, keepdims=True) + jnp.exp(unit[3] - mx)
              for p, mx, unit in zip(probs, maxes, units)]
    outs = [jnp.dot(p.astype(BF16), unit[2], preferred_element_type=F32) * (1.0 / den)
            for p, den, unit in zip(probs, denoms, units)]
    for g in range(N_KV_HEADS):
        out = outs[2 * g] + outs[2 * g + 1]
        for m in range(PAIRS):
            col = g * GROUP * HEAD_DIM + m * V7X_LANES
            o_ref[:, col:col + V7X_LANES] = out[m * WINDOW:(m + 1) * WINDOW, :].astype(o_ref.dtype)


def _attention(qkv, kv_meta, sinks, casts, *, batch, seq):
    nb = seq // WINDOW
    kcol, vcol = Q_DIM // KV_DIM, Q_DIM // KV_DIM + 1
    cast_specs, cast_shapes, cast_steps, step = [], [], [], 0
    for weight, block_rows in casts:
        spec, shape, end = _side_cast_specs(weight, block_rows, nb, first_step=step)
        cast_specs.append(spec)
        cast_shapes.append(shape)
        cast_steps.append((step, end))
        step = end
    assert step <= batch * nb

    def cur(col):
        return lambda b, j: (b * nb + j, col)

    def prev(col):
        return lambda b, j: (jnp.maximum(b * nb + j - 1, 0), col)

    return pl.pallas_call(
        functools.partial(_attn_kernel, cast_steps=tuple(cast_steps)),
        out_shape=(jax.ShapeDtypeStruct((batch * seq, Q_DIM), BF16), *cast_shapes),
        grid=(batch, nb),
        in_specs=[pl.BlockSpec(memory_space=pltpu.SMEM),
                  pl.BlockSpec((WINDOW, Q_DIM), lambda b, j: (b * nb + j, 0)),
                  pl.BlockSpec((WINDOW, KV_DIM), cur(kcol)),
                  pl.BlockSpec((WINDOW, KV_DIM), prev(kcol)),
                  pl.BlockSpec((WINDOW, KV_DIM), cur(vcol)),
                  pl.BlockSpec((WINDOW, KV_DIM), prev(vcol)),
                  pl.BlockSpec((N_META, 2 * KV_DIM), lambda b, j: (0, 0)),
                  *cast_specs],
        out_specs=(pl.BlockSpec((WINDOW, Q_DIM), lambda b, j: (b * nb + j, 0)), *cast_specs),
        compiler_params=_params(32, "arbitrary", "arbitrary"),
        name="swa_attention",
    )(sinks, qkv, qkv, qkv, qkv, qkv, kv_meta, *[weight for weight, _ in casts])


def _merge_kernel(a_ref, c_ref, wa_ref, wc_ref, bc_ref, ga_ref, gb_ref, o_ref):
    for r0 in range(0, o_ref.shape[0], DOT_ROWS):
        rs = slice(r0, r0 + DOT_ROWS)
        a = jnp.dot(a_ref[rs, :], wa_ref[...].astype(BF16), preferred_element_type=F32)
        b = jnp.dot(c_ref[rs, :], wc_ref[...].astype(BF16), preferred_element_type=F32) + bc_ref[...]
        o_ref[rs, :] = (ga_ref[rs, :].astype(F32) * a + gb_ref[rs, :].astype(F32) * b).astype(o_ref.dtype)


def _merge(attn, conv, w_ao, w_co, b_co, gates, *, tm, tn):
    assert tm % DOT_ROWS == 0 and tn == MXU_COLS
    rows, k = attn.shape
    d = w_ao.shape[1]
    nj = d // tn
    return pl.pallas_call(
        _merge_kernel,
        out_shape=jax.ShapeDtypeStruct((rows, d), BF16),
        grid=(rows // tm, nj),
        in_specs=[pl.BlockSpec((tm, k), lambda i, j: (i, 0)),
                  pl.BlockSpec((tm, k), lambda i, j: (i, 0)),
                  pl.BlockSpec((k, tn), lambda i, j: (0, j)),
                  pl.BlockSpec((k, tn), lambda i, j: (0, j)),
                  pl.BlockSpec((1, tn), lambda i, j: (0, j)),
                  pl.BlockSpec((tm, tn), lambda i, j: (i, j)),
                  pl.BlockSpec((tm, tn), lambda i, j: (i, j + nj))],
        out_specs=pl.BlockSpec((tm, tn), lambda i, j: (i, j)),
        compiler_params=_params(52, "arbitrary", "arbitrary"),
        name="gated_merge",
    )(attn, conv, w_ao, w_co, b_co.reshape(1, d), gates, gates)


def _residual_kernel(x_ref, w_ref, r_ref, o_ref):
    for r0 in range(0, o_ref.shape[0], DOT_ROWS):
        rs = slice(r0, r0 + DOT_ROWS)
        for c0 in range(0, o_ref.shape[1], MXU_COLS):
            cs = slice(c0, c0 + MXU_COLS)
            o_ref[rs, cs] = r_ref[rs, cs] + jnp.dot(x_ref[rs, :], w_ref[:, cs].astype(BF16),
                                                    preferred_element_type=F32)


def _residual_matmul(x, w, resid, *, tm, tn, vmem_mib, single_buffer_x=False):
    rows, k = x.shape
    n = w.shape[1]
    x_spec = (pl.BlockSpec((tm, k), lambda i, j: (i, 0), pipeline_mode=pl.Buffered(1))
              if single_buffer_x else pl.BlockSpec((tm, k), lambda i, j: (i, 0)))
    return pl.pallas_call(
        _residual_kernel,
        out_shape=jax.ShapeDtypeStruct((rows, n), F32),
        grid=(rows // tm, n // tn),
        in_specs=[x_spec,
                  pl.BlockSpec((k, tn), lambda i, j: (0, j)),
                  pl.BlockSpec((tm, tn), lambda i, j: (i, j))],
        out_specs=pl.BlockSpec((tm, tn), lambda i, j: (i, j)),
        compiler_params=_params(vmem_mib, "arbitrary", "arbitrary"),
        name="residual_matmul",
    )(x, w, resid)


def _ffn_up_kernel(h_ref, g_ref, wa_ref, wb_ref, o_ref, u0_ref, u1_ref, *, n_tiles, chunks):
    i, j = pl.program_id(0), pl.program_id(1)
    chunk = h_ref.shape[0]
    slots = (u0_ref, u1_ref)

    def norm(slot):
        base = pl.multiple_of(jnp.minimum(j, chunks - 1) * chunk, chunk)

        def store(r0, rows):
            slots[slot][pl.ds(pl.multiple_of(base + r0, V7X_BF16_ROWS), V7X_BF16_ROWS), :] = rows

        _rmsnorm_chunk(h_ref, g_ref, store)

    def matmul(slot):
        for r0 in range(0, o_ref.shape[0], DOT_ROWS):
            u = slots[slot][r0:r0 + DOT_ROWS, :]
            a = jnp.dot(u, wa_ref[...].astype(BF16), preferred_element_type=F32)
            b = jnp.dot(u, wb_ref[...].astype(BF16), preferred_element_type=F32)
            o_ref[r0:r0 + DOT_ROWS, :] = _swiglu(a, b).astype(o_ref.dtype)

    _next_tile_norm_body(i, j, n_tiles, chunks, norm, matmul)


def _ffn_up(h, g, w, *, ffn, tm, tn, chunks):
    rows, d = h.shape
    n_tiles, nj, chunk = rows // tm, ffn // tn, tm // chunks
    assert nj * tn == ffn and nj >= chunks
    chunk_map, w_col, out_row = _next_tile_norm_maps(n_tiles, chunks)
    return pl.pallas_call(
        functools.partial(_ffn_up_kernel, n_tiles=n_tiles, chunks=chunks),
        out_shape=jax.ShapeDtypeStruct((rows, ffn), BF16),
        grid=(n_tiles + 1, nj),
        in_specs=[pl.BlockSpec((chunk, d), chunk_map),
                  pl.BlockSpec((1, d), lambda i, j: (0, 0)),
                  pl.BlockSpec((d, tn), lambda i, j: (0, w_col(i, j))),
                  pl.BlockSpec((d, tn), lambda i, j: (0, w_col(i, j) + nj))],
        out_specs=pl.BlockSpec((tm, tn), lambda i, j: (out_row(i), w_col(i, j))),
        scratch_shapes=[pltpu.VMEM((tm, d), BF16), pltpu.VMEM((tm, d), BF16)],
        compiler_params=_params(56, "arbitrary", "arbitrary"),
        name="ffn_gate_up",
    )(h, g.reshape(1, d), w, w)


def _rope_tables(first_pos, length):
    pos = jnp.arange(first_pos, first_pos + length, dtype=F32)
    inv_freq = ROPE_THETA ** (-jnp.arange(0, HEAD_DIM, 2, dtype=F32) / HEAD_DIM)
    ang = pos[:, None] * inv_freq[None, :]
    cos, sin = jnp.cos(ang), jnp.sin(ang)
    reps = V7X_LANES // HEAD_DIM
    return (jnp.tile(jnp.concatenate([cos, cos], axis=1), (1, reps)),
            jnp.tile(jnp.concatenate([-sin, sin], axis=1), (1, reps)))


def kernel(x, meta_tokens, mix_norm_g, w_in, b_in, attn_sinks, conv_w, conv_b, conv_ln_g, conv_ln_b,
           w_attn_o, w_conv_o, b_conv_o, w_out, ffn_norm_g, w_gate_up, w_down, final_norm_g):
    batch, seq, d = x.shape
    depth = w_in.shape[0]
    assert depth == 1, "one layer: the meta rows' outputs are never needed"
    conv_dim = conv_w.shape[-1]
    ffn = w_down.shape[1]
    in_dim = w_in.shape[-1]
    rows = batch * seq
    col_conv = Q_DIM + 2 * KV_DIM
    col_gate = col_conv + 2 * conv_dim
    assert in_dim == col_gate + 2 * d

    w_in2 = w_in.reshape(d, in_dim)
    b_in2 = b_in.reshape(1, in_dim)
    h0 = x.reshape(rows, d)
    tm = 1024
    qkv_tn = 512
    qkv_nj = (Q_DIM + 2 * KV_DIM) // qkv_tn

    cos, sin_signed = _rope_tables(N_META, seq)
    cos_m, sin_m = _rope_tables(0, N_META)

    qkv, u = _qkv_norm_proj(h0, mix_norm_g[0], w_in2, b_in2, cos, sin_signed, tm=tm, tn=qkv_tn,
                            nj=qkv_nj, chunks=qkv_nj - 1, rows_per_seq=seq)
    u_meta = _rmsnorm(meta_tokens, mix_norm_g[0], BF16, tm=N_META)
    kv_meta = _qkv_proj(u_meta, w_in2, b_in2, cos_m, sin_m, tm=N_META, tn=qkv_tn, j0=Q_DIM // qkv_tn, nj=1)
    glu = functools.partial(_pair_proj, w=w_in2, b=b_in2, col_a=col_conv, col_b=col_conv + conv_dim,
                            width=conv_dim, tn=256, combine=_glu, out_dtype=F32, name="conv_glu_proj")
    c, w_down_bf16 = glu(u, tm=tm, vmem_mib=52, cast=(w_down[0], 128))
    c_meta = glu(u_meta, tm=N_META, vmem_mib=32)

    gates, conv, w_gate_up_bf16 = _gates_conv(
        u, w_in2, b_in2, c, c_meta, conv_w.reshape(CONV_WIDTH, conv_dim), conv_b[0], conv_ln_g[0],
        conv_ln_b[0], (w_gate_up[0], V7X_BF16_ROWS), col0=col_gate, width=2 * d, tm=2 * tm, tn=MXU_COLS,
        rows_per_seq=seq)
    attn, w_ao_bf16, w_co_bf16, w_out_bf16 = _attention(
        qkv, kv_meta, attn_sinks[0], [(w_attn_o[0], 64), (w_conv_o[0], 64), (w_out[0], 64)],
        batch=batch, seq=seq)

    merged = _merge(attn, conv, w_ao_bf16, w_co_bf16, b_conv_o[0], gates, tm=2 * tm, tn=MXU_COLS)
    h1 = _residual_matmul(merged, w_out_bf16, h0, tm=2 * tm, tn=MXU_COLS, vmem_mib=52)

    act = _ffn_up(h1, ffn_norm_g[0], w_gate_up_bf16, ffn=ffn, tm=2 * tm, tn=256, chunks=16)
    h2 = _residual_matmul(act, w_down_bf16, h1, tm=tm, tn=512, vmem_mib=58, single_buffer_x=True)

    y = _rmsnorm(h2, final_norm_g, x.dtype, tm=256)
    return y.reshape(batch, seq, d)
```

```python
import functools

import jax
import jax.numpy as jnp
from jax import lax
from jax.experimental import pallas as pl
from jax.experimental.pallas import tpu as pltpu

N_META = 16
HEAD_DIM = 64
N_Q_HEADS = 32
N_KV_HEADS = 4
GROUP = N_Q_HEADS // N_KV_HEADS
WINDOW = 128
Q_DIM = N_Q_HEADS * HEAD_DIM
KV_DIM = N_KV_HEADS * HEAD_DIM
CONV_WIDTH = 31
ROPE_THETA = 10000.0
EPS = 1e-6

V7X_LANES = 128
V7X_BF16_ROWS = 16
V7X_VMEM_BYTES = 64 * 1024 * 1024
MXU_COLS = 256
DOT_ROWS = 1024
MIB = 1024 * 1024

F32 = jnp.float32
BF16 = jnp.bfloat16


def _params(vmem_mib, *sem):
    assert vmem_mib * MIB < V7X_VMEM_BYTES
    return pltpu.CompilerParams(dimension_semantics=sem, vmem_limit_bytes=vmem_mib * MIB)


def _after(x, dep):
    if dep is None:
        return x
    zero = lax.shift_right_logical(
        lax.shift_right_logical(lax.bitcast_convert_type(dep, jnp.uint32), jnp.uint32(16)), jnp.uint32(16))
    return lax.bitcast_convert_type(lax.bitcast_convert_type(x, jnp.uint32) | zero, x.dtype)


def _rmsnorm_kernel(x_ref, g_ref, o_ref):
    x = x_ref[...]
    ms = jnp.mean(x * x, axis=-1, keepdims=True)
    o_ref[...] = (x * lax.rsqrt(ms + EPS) * g_ref[...]).astype(o_ref.dtype)


def _rmsnorm(x, g, out_dtype, tm):
    rows, d = x.shape
    vmem_mib = max(16, 6 * tm * d * 4 // MIB)
    return pl.pallas_call(
        _rmsnorm_kernel,
        out_shape=jax.ShapeDtypeStruct((rows, d), out_dtype),
        grid=(rows // tm,),
        in_specs=[pl.BlockSpec((tm, d), lambda i: (i, 0)),
                  pl.BlockSpec((1, d), lambda i: (0, 0))],
        out_specs=pl.BlockSpec((tm, d), lambda i: (i, 0)),
        compiler_params=_params(vmem_mib, "arbitrary"),
        name="rmsnorm",
    )(x, g.reshape(1, d))


def _rmsnorm_chunk(x_ref, g_ref, store):
    dep = None
    for r0 in range(0, x_ref.shape[0], V7X_BF16_ROWS):
        x = x_ref[r0:r0 + V7X_BF16_ROWS, :]
        x = jnp.concatenate([_after(x[:, :V7X_LANES], dep), x[:, V7X_LANES:]], axis=1)
        ms = jnp.mean(x * x, axis=-1, keepdims=True)
        y = x * lax.rsqrt(ms + EPS) * g_ref[...]
        store(r0, y.astype(BF16))
        dep = y[:, :V7X_LANES]


def _next_tile_norm_maps(n_tiles, chunks):
    last = n_tiles * chunks - 1

    def chunk_map(i, j):
        return (jnp.where(i >= n_tiles, last, i * chunks + jnp.minimum(j, chunks - 1)), 0)

    def w_col(i, j):
        return jnp.where(i == 0, 0, j)

    def out_row(i):
        return jnp.maximum(i - 1, 0)

    return chunk_map, w_col, out_row


def _next_tile_norm_body(i, j, n_tiles, chunks, norm, matmul):
    do_norm = jnp.logical_and(j < chunks, i < n_tiles)

    @pl.when(jnp.logical_and(i == 0, do_norm))
    def _():
        norm(0)

    for slot in (0, 1):
        mine = jnp.logical_and(i > 0, i % 2 == slot)

        @pl.when(jnp.logical_and(mine, do_norm))
        def _():
            norm(slot)
            matmul(1 - slot)

        @pl.when(jnp.logical_and(mine, jnp.logical_not(do_norm)))
        def _():
            matmul(1 - slot)


def _rope(z, cos, sin_signed):
    lane = lax.broadcasted_iota(jnp.int32, (z.shape[0], V7X_LANES), 1)
    first_half = (lane & (HEAD_DIM - 1)) < HEAD_DIM // 2
    out = []
    for t in range(z.shape[1] // V7X_LANES):
        zt = z[:, t * V7X_LANES:(t + 1) * V7X_LANES]
        partner = jnp.where(first_half, pltpu.roll(zt, V7X_LANES - HEAD_DIM // 2, 1),
                            pltpu.roll(zt, HEAD_DIM // 2, 1))
        out.append(zt * cos + partner * sin_signed)
    return jnp.concatenate(out, axis=1)


def _qkv_tile(u, w_ref, b_ref, cos, sin_signed, o_ref, j, tn):
    assert tn == 2 * KV_DIM and KV_DIM == MXU_COLS
    is_q = j < Q_DIM // tn
    scale = jnp.where(is_q, HEAD_DIM ** -0.5, 1.0).astype(F32)
    for c0 in range(0, tn, MXU_COLS):
        cs = slice(c0, c0 + MXU_COLS)
        z = jnp.dot(u, w_ref[:, cs].astype(BF16), preferred_element_type=F32) + b_ref[:, cs]
        roped = _rope(z, cos, sin_signed)
        out = roped if c0 == 0 else jnp.where(is_q, roped, z)
        o_ref[:, cs] = (out * scale).astype(BF16)


def _qkv_kernel(u_ref, w_ref, b_ref, cos_ref, sin_ref, o_ref, *, tn, j0):
    _qkv_tile(u_ref[...], w_ref, b_ref, cos_ref[...], sin_ref[...], o_ref, pl.program_id(1) + j0, tn)


def _qkv_proj(u, w_in, b_in, cos, sin_signed, *, tm, tn, j0, nj):
    rows, d = u.shape
    assert rows == tm
    return pl.pallas_call(
        functools.partial(_qkv_kernel, tn=tn, j0=j0),
        out_shape=jax.ShapeDtypeStruct((rows, nj * tn), BF16),
        grid=(1, nj),
        in_specs=[pl.BlockSpec((tm, d), lambda i, j: (0, 0)),
                  pl.BlockSpec((d, tn), lambda i, j: (0, j + j0)),
                  pl.BlockSpec((1, tn), lambda i, j: (0, j + j0)),
                  pl.BlockSpec((tm, V7X_LANES), lambda i, j: (0, 0)),
                  pl.BlockSpec((tm, V7X_LANES), lambda i, j: (0, 0))],
        out_specs=pl.BlockSpec((tm, tn), lambda i, j: (0, j)),
        compiler_params=_params(32, "arbitrary", "arbitrary"),
        name="qkv_proj_meta",
    )(u, w_in, b_in, cos, sin_signed)


def _qkv_norm_kernel(x_ref, g_ref, w_ref, b_ref, cos_ref, sin_ref, o_ref, uo_ref, u0_ref, u1_ref,
                     *, tn, n_tiles, chunks):
    i, j = pl.program_id(0), pl.program_id(1)
    chunk = x_ref.shape[0]
    slots = (u0_ref, u1_ref)

    def norm(slot):
        base = pl.multiple_of(jnp.minimum(j, chunks - 1) * chunk, chunk)

        def store(r0, rows):
            slots[slot][pl.ds(pl.multiple_of(base + r0, V7X_BF16_ROWS), V7X_BF16_ROWS), :] = rows
            uo_ref[r0:r0 + V7X_BF16_ROWS, :] = rows

        _rmsnorm_chunk(x_ref, g_ref, store)

    def matmul(slot):
        _qkv_tile(slots[slot][...], w_ref, b_ref, cos_ref[...], sin_ref[...], o_ref, j, tn)

    _next_tile_norm_body(i, j, n_tiles, chunks, norm, matmul)


def _qkv_norm_proj(x, g, w_in, b_in, cos, sin_signed, *, tm, tn, nj, chunks, rows_per_seq):
    rows, d = x.shape
    n_tiles, seq_tiles, chunk = rows // tm, rows_per_seq // tm, tm // chunks
    chunk_map, w_col, out_row = _next_tile_norm_maps(n_tiles, chunks)
    return pl.pallas_call(
        functools.partial(_qkv_norm_kernel, tn=tn, n_tiles=n_tiles, chunks=chunks),
        out_shape=(jax.ShapeDtypeStruct((rows, nj * tn), BF16), jax.ShapeDtypeStruct((rows, d), BF16)),
        grid=(n_tiles + 1, nj),
        in_specs=[pl.BlockSpec((chunk, d), chunk_map),
                  pl.BlockSpec((1, d), lambda i, j: (0, 0)),
                  pl.BlockSpec((d, tn), lambda i, j: (0, w_col(i, j))),
                  pl.BlockSpec((1, tn), lambda i, j: (0, w_col(i, j))),
                  pl.BlockSpec((tm, V7X_LANES), lambda i, j: (out_row(i) % seq_tiles, 0)),
                  pl.BlockSpec((tm, V7X_LANES), lambda i, j: (out_row(i) % seq_tiles, 0))],
        out_specs=(pl.BlockSpec((tm, tn), lambda i, j: (out_row(i), w_col(i, j))),
                   pl.BlockSpec((chunk, d), chunk_map)),
        scratch_shapes=[pltpu.VMEM((tm, d), BF16), pltpu.VMEM((tm, d), BF16)],
        compiler_params=_params(58, "arbitrary", "arbitrary"),
        name="qkv_proj",
    )(x, g.reshape(1, d), w_in, b_in, cos, sin_signed)


def _sigmoid(x):
    return 0.5 * jnp.tanh(0.5 * x) + 0.5


def _glu(a, g):
    return a * _sigmoid(g)


def _swiglu(g, up):
    return g * _sigmoid(g) * up


def _side_cast_specs(weight, block_rows, nj, first_step=0):
    w_rows, w_cols = weight.shape
    assert w_rows % block_rows == 0 and block_rows % V7X_BF16_ROWS == 0
    last = w_rows // block_rows - 1
    spec = pl.BlockSpec((block_rows, w_cols),
                        lambda i, j: (jnp.clip(i * nj + j - first_step, 0, last), 0))
    return spec, jax.ShapeDtypeStruct((w_rows, w_cols), BF16), first_step + last + 1


def _pair_kernel(u_ref, wa_ref, wb_ref, ba_ref, bb_ref, *rest, combine):
    if len(rest) == 3:
        cast_in_ref, o_ref, cast_out_ref = rest
        cast_out_ref[...] = cast_in_ref[...].astype(BF16)
    else:
        (o_ref,) = rest
    a = jnp.dot(u_ref[...], wa_ref[...].astype(BF16), preferred_element_type=F32) + ba_ref[...]
    b = jnp.dot(u_ref[...], wb_ref[...].astype(BF16), preferred_element_type=F32) + bb_ref[...]
    o_ref[...] = combine(a, b).astype(o_ref.dtype)


def _pair_proj(u, w, b, *, col_a, col_b, width, tm, tn, combine, out_dtype, name, vmem_mib, cast=None):
    rows, d = u.shape
    ja, jb = col_a // tn, col_b // tn
    nj = width // tn
    assert ja * tn == col_a and jb * tn == col_b and nj * tn == width
    in_specs = [pl.BlockSpec((tm, d), lambda i, j: (i, 0)),
                pl.BlockSpec((d, tn), lambda i, j: (0, j + ja)),
                pl.BlockSpec((d, tn), lambda i, j: (0, j + jb)),
                pl.BlockSpec((1, tn), lambda i, j: (0, j + ja)),
                pl.BlockSpec((1, tn), lambda i, j: (0, j + jb))]
    args = [u, w, w, b, b]
    out_shape = jax.ShapeDtypeStruct((rows, width), out_dtype)
    out_specs = pl.BlockSpec((tm, tn), lambda i, j: (i, j))
    if cast is not None:
        spec, shape, steps = _side_cast_specs(cast[0], cast[1], nj)
        assert steps <= (rows // tm) * nj
        in_specs.append(spec)
        args.append(cast[0])
        out_shape, out_specs = (out_shape, shape), (out_specs, spec)
    return pl.pallas_call(
        functools.partial(_pair_kernel, combine=combine),
        out_shape=out_shape,
        grid=(rows // tm, nj),
        in_specs=in_specs,
        out_specs=out_specs,
        compiler_params=_params(vmem_mib, "arbitrary", "arbitrary"),
        name=name,
    )(*args)


CONV_HALO = 32
CONV_ROWS = 8
CONV_CHAINS = 2


def _conv_chunk(first, c_ref, cm_ref, w_ref, cb_ref, g_ref, b_ref, o_ref, ext_ref, y_ref):
    lane_tiles, rows = ext_ref.shape[0], c_ref.shape[0]
    lead = CONV_HALO - (CONV_WIDTH - 1)
    groups = rows // CONV_ROWS

    @pl.when(first)
    def _():
        for ct in range(lane_tiles):
            ext_ref[ct, 0:CONV_HALO - N_META, :] = jnp.zeros((CONV_HALO - N_META, V7X_LANES), F32)
            ext_ref[ct, CONV_HALO - N_META:CONV_HALO, :] = cm_ref[:, ct * V7X_LANES:(ct + 1) * V7X_LANES]

    @pl.when(jnp.logical_not(first))
    def _():
        for ct in range(lane_tiles):
            ext_ref[ct, 0:CONV_HALO, :] = ext_ref[ct, rows:rows + CONV_HALO, :]

    for ct in range(lane_tiles):
        ext_ref[ct, CONV_HALO:, :] = c_ref[:, ct * V7X_LANES:(ct + 1) * V7X_LANES]

    prev = [None] * CONV_CHAINS
    row_sum = [None] * groups
    n = 0
    for ct in range(lane_tiles):
        ls = slice(ct * V7X_LANES, (ct + 1) * V7X_LANES)
        for j in range(groups):
            acc = _after(jnp.broadcast_to(cb_ref[:, ls], (CONV_ROWS, V7X_LANES)), prev[n % CONV_CHAINS])
            for k in range(CONV_WIDTH):
                tap = jnp.broadcast_to(w_ref[k:k + 1, ls], (CONV_ROWS, V7X_LANES))
                r0 = j * CONV_ROWS + lead + k
                acc = acc + ext_ref[ct, r0:r0 + CONV_ROWS, :] * tap
            y_ref[j * CONV_ROWS:(j + 1) * CONV_ROWS, ls] = acc
            row_sum[j] = acc if row_sum[j] is None else row_sum[j] + acc
            prev[n % CONV_CHAINS] = acc
            n += 1

    inv_n = 1.0 / (lane_tiles * V7X_LANES)
    done = None
    for j in range(groups):
        rs = slice(j * CONV_ROWS, (j + 1) * CONV_ROWS)
        mu = jnp.sum(_after(row_sum[j], done), axis=1, keepdims=True) * inv_n
        parts = [None, None]
        for ct in range(lane_tiles):
            d = y_ref[rs, ct * V7X_LANES:(ct + 1) * V7X_LANES] - mu
            parts[ct % 2] = d * d if parts[ct % 2] is None else parts[ct % 2] + d * d
        var = jnp.sum(parts[0] + parts[1], axis=1, keepdims=True) * inv_n
        rstd = lax.rsqrt(var + EPS)
        for ct in range(lane_tiles):
            ls = slice(ct * V7X_LANES, (ct + 1) * V7X_LANES)
            yn = (y_ref[rs, ls] - mu) * rstd * g_ref[:, ls] + b_ref[:, ls]
            done = yn * _sigmoid(yn)
            y_ref[rs, ls] = done
    o_ref[...] = y_ref[...].astype(o_ref.dtype)


def _gates_conv_kernel(u_ref, w_ref, b_ref, c_ref, cm_ref, cw_ref, cb_ref, g_ref, bb_ref, cast_in_ref,
                       o_ref, co_ref, cast_out_ref, ext_ref, y_ref, *, seq_tiles):
    i, j = pl.program_id(0), pl.program_id(1)
    cast_out_ref[...] = cast_in_ref[...].astype(BF16)
    first = jnp.logical_and(i % seq_tiles == 0, j == 0)
    _conv_chunk(first, c_ref, cm_ref, cw_ref, cb_ref, g_ref, bb_ref, co_ref, ext_ref, y_ref)
    for r0 in range(0, o_ref.shape[0], DOT_ROWS):
        rs = slice(r0, r0 + DOT_ROWS)
        for c0 in range(0, o_ref.shape[1], MXU_COLS):
            cs = slice(c0, c0 + MXU_COLS)
            z = jnp.dot(u_ref[rs, :], w_ref[:, cs].astype(BF16), preferred_element_type=F32) + b_ref[:, cs]
            o_ref[rs, cs] = _sigmoid(z).astype(o_ref.dtype)


def _gates_conv(u, w, b, c, c_meta, conv_w, conv_b, ln_g, ln_b, cast, *, col0, width, tm, tn, rows_per_seq):
    rows, d = u.shape
    channels = c.shape[1]
    j0, nj = col0 // tn, width // tn
    assert j0 * tn == col0 and nj * tn == width and tm % nj == 0
    chunk = tm // nj
    assert chunk % V7X_BF16_ROWS == 0 and chunk >= CONV_HALO
    small = lambda i, j: (0, 0)
    cast_spec, cast_shape, cast_steps = _side_cast_specs(cast[0], cast[1], nj)
    assert cast_steps <= (rows // tm) * nj
    return pl.pallas_call(
        functools.partial(_gates_conv_kernel, seq_tiles=rows_per_seq // tm),
        out_shape=(jax.ShapeDtypeStruct((rows, width), BF16), jax.ShapeDtypeStruct((rows, channels), BF16),
                   cast_shape),
        grid=(rows // tm, nj),
        in_specs=[pl.BlockSpec((tm, d), lambda i, j: (i, 0)),
                  pl.BlockSpec((d, tn), lambda i, j: (0, j + j0)),
                  pl.BlockSpec((1, tn), lambda i, j: (0, j + j0)),
                  pl.BlockSpec((chunk, channels), lambda i, j: (i * nj + j, 0)),
                  pl.BlockSpec((N_META, channels), small),
                  pl.BlockSpec((CONV_WIDTH, channels), small),
                  pl.BlockSpec((1, channels), small),
                  pl.BlockSpec((1, channels), small),
                  pl.BlockSpec((1, channels), small),
                  cast_spec],
        out_specs=(pl.BlockSpec((tm, tn), lambda i, j: (i, j)),
                   pl.BlockSpec((chunk, channels), lambda i, j: (i * nj + j, 0)),
                   cast_spec),
        scratch_shapes=[pltpu.VMEM((channels // V7X_LANES, chunk + CONV_HALO, V7X_LANES), F32),
                        pltpu.VMEM((chunk, channels), F32)],
        compiler_params=_params(52, "arbitrary", "arbitrary"),
        name="gates_conv",
    )(u, w, b, c, c_meta, conv_w, conv_b.reshape(1, channels), ln_g.reshape(1, channels),
      ln_b.reshape(1, channels), cast[0])


ATT_KEYS = 3 * WINDOW
PAIRS = GROUP // 2


def _attn_kernel(sink_ref, q_ref, kc_ref, kp_ref, vc_ref, vp_ref, kvm_ref, *rest, cast_steps):
    n_cast = len(cast_steps)
    o_ref = rest[n_cast]
    step = pl.program_id(0) * pl.num_programs(1) + pl.program_id(1)
    for cast_in_ref, cast_out_ref, (lo, hi) in zip(rest[:n_cast], rest[n_cast + 1:], cast_steps):
        @pl.when(jnp.logical_and(step >= lo, step < hi))
        def _():
            cast_out_ref[...] = cast_in_ref[...].astype(BF16)
    first_block = pl.program_id(1) == 0
    rows = PAIRS * WINDOW
    r = lax.broadcasted_iota(jnp.int32, (rows, ATT_KEYS), 0) & (WINDOW - 1)
    c = lax.broadcasted_iota(jnp.int32, (rows, ATT_KEYS), 1)
    c_min = jnp.where(first_block, WINDOW, 0)
    band = (c > r) & (c <= r + WINDOW) & (c >= c_min)
    meta = (c >= 2 * WINDOW) & (c < 2 * WINDOW + N_META)
    bias = jnp.where(band | meta, 0.0, -jnp.inf).astype(F32)

    row_pair = lax.shift_right_logical(lax.broadcasted_iota(jnp.int32, (rows, 1), 0), WINDOW.bit_length() - 1)
    lane = lax.broadcasted_iota(jnp.int32, (ATT_KEYS, V7X_LANES), 1)
    low = lane < HEAD_DIM
    zero_pad = jnp.zeros((WINDOW - N_META, V7X_LANES), F32)

    units = []
    for g in range(N_KV_HEADS):
        tile, upper = g // 2, g % 2
        sl = slice(tile * V7X_LANES, (tile + 1) * V7X_LANES)

        def halves(prev_ref, cur_ref, meta_cols):
            x = jnp.concatenate([prev_ref[:, sl].astype(F32), cur_ref[:, sl].astype(F32),
                                 kvm_ref[:, meta_cols].astype(F32), zero_pad], axis=0)
            swapped = pltpu.roll(x, HEAD_DIM, 1)
            lo_src, hi_src = (swapped, x) if upper else (x, swapped)
            return (jnp.where(low, lo_src, 0.0).astype(BF16), jnp.where(low, 0.0, hi_src).astype(BF16))

        k_even, k_odd = halves(kp_ref, kc_ref, slice(tile * V7X_LANES, (tile + 1) * V7X_LANES))
        v_even, v_odd = halves(vp_ref, vc_ref, slice(KV_DIM + tile * V7X_LANES, KV_DIM + (tile + 1) * V7X_LANES))

        qp = jnp.concatenate(
            [q_ref[:, g * GROUP * HEAD_DIM + m * V7X_LANES: g * GROUP * HEAD_DIM + (m + 1) * V7X_LANES]
             for m in range(PAIRS)], axis=0)

        for parity, (kz, vz) in enumerate(((k_even, v_even), (k_odd, v_odd))):
            sink = jnp.zeros((rows, 1), F32)
            for m in range(PAIRS):
                sink = jnp.where(row_pair == m, sink_ref[g * GROUP + 2 * m + parity], sink)
            units.append((qp, kz, vz, sink))

    scores = [lax.dot_general(qp, kz, (((1,), (1,)), ((), ())), preferred_element_type=F32) + bias
              for qp, kz, _, _ in units]
    maxes = [jnp.maximum(jnp.max(s, axis=1, keepdims=True), unit[3]) for s, unit in zip(scores, units)]
    probs = [jnp.exp(s - mx) for s, mx in zip(scores, maxes)]
    denoms = [jnp.sum(p, axis=1, keepdims=True) + jnp.exp(unit[3] - mx)
              for p, mx, unit in zip(probs, maxes, units)]
    outs = [jnp.dot(p.astype(BF16), unit[2], preferred_element_type=F32) * (1.0 / den)
            for p, den, unit in zip(probs, denoms, units)]
    for g in range(N_KV_HEADS):
        out = outs[2 * g] + outs[2 * g + 1]
        for m in range(PAIRS):
            col = g * GROUP * HEAD_DIM + m * V7X_LANES
            o_ref[:, col:col + V7X_LANES] = out[m * WINDOW:(m + 1) * WINDOW, :].astype(o_ref.dtype)


def _attention(qkv, kv_meta, sinks, casts, *, batch, seq):
    nb = seq // WINDOW
    kcol, vcol = Q_DIM // KV_DIM, Q_DIM // KV_DIM + 1
    cast_specs, cast_shapes, cast_steps, step = [], [], [], 0
    for weight, block_rows in casts:
        spec, shape, end = _side_cast_specs(weight, block_rows, nb, first_step=step)
        cast_specs.append(spec)
        cast_shapes.append(shape)
        cast_steps.append((step, end))
        step = end
    assert step <= batch * nb

    def cur(col):
        return lambda b, j: (b * nb + j, col)

    def prev(col):
        return lambda b, j: (jnp.maximum(b * nb + j - 1, 0), col)

    return pl.pallas_call(
        functools.partial(_attn_kernel, cast_steps=tuple(cast_steps)),
        out_shape=(jax.ShapeDtypeStruct((batch * seq, Q_DIM), BF16), *cast_shapes),
        grid=(batch, nb),
        in_specs=[pl.BlockSpec(memory_space=pltpu.SMEM),
                  pl.BlockSpec((WINDOW, Q_DIM), lambda b, j: (b * nb + j, 0)),
                  pl.BlockSpec((WINDOW, KV_DIM), cur(kcol)),
                  pl.BlockSpec((WINDOW, KV_DIM), prev(kcol)),
                  pl.BlockSpec((WINDOW, KV_DIM), cur(vcol)),
                  pl.BlockSpec((WINDOW, KV_DIM), prev(vcol)),
                  pl.BlockSpec((N_META, 2 * KV_DIM), lambda b, j: (0, 0)),
                  *cast_specs],
        out_specs=(pl.BlockSpec((WINDOW, Q_DIM), lambda b, j: (b * nb + j, 0)), *cast_specs),
        compiler_params=_params(32, "arbitrary", "arbitrary"),
        name="swa_attention",
    )(sinks, qkv, qkv, qkv, qkv, qkv, kv_meta, *[weight for weight, _ in casts])


def _merge_kernel(a_ref, c_ref, wa_ref, wc_ref, bc_ref, ga_ref, gb_ref, o_ref):
    for r0 in range(0, o_ref.shape[0], DOT_ROWS):
        rs = slice(r0, r0 + DOT_ROWS)
        a = jnp.dot(a_ref[rs, :], wa_ref[...].astype(BF16), preferred_element_type=F32)
        b = jnp.dot(c_ref[rs, :], wc_ref[...].astype(BF16), preferred_element_type=F32) + bc_ref[...]
        o_ref[rs, :] = (ga_ref[rs, :].astype(F32) * a + gb_ref[rs, :].astype(F32) * b).astype(o_ref.dtype)


def _merge(attn, conv, w_ao, w_co, b_co, gates, *, tm, tn):
    assert tm % DOT_ROWS == 0 and tn == MXU_COLS
    rows, k = attn.shape
    d = w_ao.shape[1]
    nj = d // tn
    return pl.pallas_call(
        _merge_kernel,
        out_shape=jax.ShapeDtypeStruct((rows, d), BF16),
        grid=(rows // tm, nj),
        in_specs=[pl.BlockSpec((tm, k), lambda i, j: (i, 0)),
                  pl.BlockSpec((tm, k), lambda i, j: (i, 0)),
                  pl.BlockSpec((k, tn), lambda i, j: (0, j)),
                  pl.BlockSpec((k, tn), lambda i, j: (0, j)),
                  pl.BlockSpec((1, tn), lambda i, j: (0, j)),
                  pl.BlockSpec((tm, tn), lambda i, j: (i, j)),
                  pl.BlockSpec((tm, tn), lambda i, j: (i, j + nj))],
        out_specs=pl.BlockSpec((tm, tn), lambda i, j: (i, j)),
        compiler_params=_params(52, "arbitrary", "arbitrary"),
        name="gated_merge",
    )(attn, conv, w_ao, w_co, b_co.reshape(1, d), gates, gates)


def _residual_kernel(x_ref, w_ref, r_ref, o_ref):
    for r0 in range(0, o_ref.shape[0], DOT_ROWS):
        rs = slice(r0, r0 + DOT_ROWS)
        for c0 in range(0, o_ref.shape[1], MXU_COLS):
            cs = slice(c0, c0 + MXU_COLS)
            o_ref[rs, cs] = r_ref[rs, cs] + jnp.dot(x_ref[rs, :], w_ref[:, cs].astype(BF16),
                                                    preferred_element_type=F32)


def _residual_matmul(x, w, resid, *, tm, tn, vmem_mib, single_buffer_x=False):
    rows, k = x.shape
    n = w.shape[1]
    x_spec = (pl.BlockSpec((tm, k), lambda i, j: (i, 0), pipeline_mode=pl.Buffered(1))
              if single_buffer_x else pl.BlockSpec((tm, k), lambda i, j: (i, 0)))
    return pl.pallas_call(
        _residual_kernel,
        out_shape=jax.ShapeDtypeStruct((rows, n), F32),
        grid=(rows // tm, n // tn),
        in_specs=[x_spec,
                  pl.BlockSpec((k, tn), lambda i, j: (0, j)),
                  pl.BlockSpec((tm, tn), lambda i, j: (i, j))],
        out_specs=pl.BlockSpec((tm, tn), lambda i, j: (i, j)),
        compiler_params=_params(vmem_mib, "arbitrary", "arbitrary"),
        name="residual_matmul",
    )(x, w, resid)


def _ffn_up_kernel(h_ref, g_ref, wa_ref, wb_ref, o_ref, u0_ref, u1_ref, *, n_tiles, chunks):
    i, j = pl.program_id(0), pl.program_id(1)
    chunk = h_ref.shape[0]
    slots = (u0_ref, u1_ref)

    def norm(slot):
        base = pl.multiple_of(jnp.minimum(j, chunks - 1) * chunk, chunk)

        def store(r0, rows):
            slots[slot][pl.ds(pl.multiple_of(base + r0, V7X_BF16_ROWS), V7X_BF16_ROWS), :] = rows

        _rmsnorm_chunk(h_ref, g_ref, store)

    def matmul(slot):
        for r0 in range(0, o_ref.shape[0], DOT_ROWS):
            u = slots[slot][r0:r0 + DOT_ROWS, :]
            a = jnp.dot(u, wa_ref[...].astype(BF16), preferred_element_type=F32)
            b = jnp.dot(u, wb_ref[...].astype(BF16), preferred_element_type=F32)
            o_ref[r0:r0 + DOT_ROWS, :] = _swiglu(a, b).astype(o_ref.dtype)

    _next_tile_norm_body(i, j, n_tiles, chunks, norm, matmul)


def _ffn_up(h, g, w, *, ffn, tm, tn, chunks):
    rows, d = h.shape
    n_tiles, nj, chunk = rows // tm, ffn // tn, tm // chunks
    assert nj * tn == ffn and nj >= chunks
    chunk_map, w_col, out_row = _next_tile_norm_maps(n_tiles, chunks)
    return pl.pallas_call(
        functools.partial(_ffn_up_kernel, n_tiles=n_tiles, chunks=chunks),
        out_shape=jax.ShapeDtypeStruct((rows, ffn), BF16),
        grid=(n_tiles + 1, nj),
        in_specs=[pl.BlockSpec((chunk, d), chunk_map),
                  pl.BlockSpec((1, d), lambda i, j: (0, 0)),
                  pl.BlockSpec((d, tn), lambda i, j: (0, w_col(i, j))),
                  pl.BlockSpec((d, tn), lambda i, j: (0, w_col(i, j) + nj))],
        out_specs=pl.BlockSpec((tm, tn), lambda i, j: (out_row(i), w_col(i, j))),
        scratch_shapes=[pltpu.VMEM((tm, d), BF16), pltpu.VMEM((tm, d), BF16)],
        compiler_params=_params(56, "arbitrary", "arbitrary"),
        name="ffn_gate_up",
    )(h, g.reshape(1, d), w, w)


def _rope_tables(first_pos, length):
    pos = jnp.arange(first_pos, first_pos + length, dtype=F32)
    inv_freq = ROPE_THETA ** (-jnp.arange(0, HEAD_DIM, 2, dtype=F32) / HEAD_DIM)
    ang = pos[:, None] * inv_freq[None, :]
    cos, sin = jnp.cos(ang), jnp.sin(ang)
    reps = V7X_LANES // HEAD_DIM
    return (jnp.tile(jnp.concatenate([cos, cos], axis=1), (1, reps)),
            jnp.tile(jnp.concatenate([-sin, sin], axis=1), (1, reps)))


def kernel(x, meta_tokens, mix_norm_g, w_in, b_in, attn_sinks, conv_w, conv_b, conv_ln_g, conv_ln_b,
           w_attn_o, w_conv_o, b_conv_o, w_out, ffn_norm_g, w_gate_up, w_down, final_norm_g):
    batch, seq, d = x.shape
    depth = w_in.shape[0]
    assert depth == 1, "one layer: the meta rows' outputs are never needed"
    conv_dim = conv_w.shape[-1]
    ffn = w_down.shape[1]
    in_dim = w_in.shape[-1]
    rows = batch * seq
    col_conv = Q_DIM + 2 * KV_DIM
    col_gate = col_conv + 2 * conv_dim
    assert in_dim == col_gate + 2 * d

    w_in2 = w_in.reshape(d, in_dim)
    b_in2 = b_in.reshape(1, in_dim)
    h0 = x.reshape(rows, d)
    tm = 1024
    qkv_tn = 512
    qkv_nj = (Q_DIM + 2 * KV_DIM) // qkv_tn

    cos, sin_signed = _rope_tables(N_META, seq)
    cos_m, sin_m = _rope_tables(0, N_META)

    qkv, u = _qkv_norm_proj(h0, mix_norm_g[0], w_in2, b_in2, cos, sin_signed, tm=tm, tn=qkv_tn,
                            nj=qkv_nj, chunks=qkv_nj - 1, rows_per_seq=seq)
    u_meta = _rmsnorm(meta_tokens, mix_norm_g[0], BF16, tm=N_META)
    kv_meta = _qkv_proj(u_meta, w_in2, b_in2, cos_m, sin_m, tm=N_META, tn=qkv_tn, j0=Q_DIM // qkv_tn, nj=1)
    glu = functools.partial(_pair_proj, w=w_in2, b=b_in2, col_a=col_conv, col_b=col_conv + conv_dim,
                            width=conv_dim, tn=256, combine=_glu, out_dtype=F32, name="conv_glu_proj")
    c, w_down_bf16 = glu(u, tm=tm, vmem_mib=52, cast=(w_down[0], 128))
    c_meta = glu(u_meta, tm=N_META, vmem_mib=32)

    gates, conv, w_gate_up_bf16 = _gates_conv(
        u, w_in2, b_in2, c, c_meta, conv_w.reshape(CONV_WIDTH, conv_dim), conv_b[0], conv_ln_g[0],
        conv_ln_b[0], (w_gate_up[0], V7X_BF16_ROWS), col0=col_gate, width=2 * d, tm=2 * tm, tn=MXU_COLS,
        rows_per_seq=seq)
    attn, w_ao_bf16, w_co_bf16, w_out_bf16 = _attention(
        qkv, kv_meta, attn_sinks[0], [(w_attn_o[0], 64), (w_conv_o[0], 64), (w_out[0], 64)],
        batch=batch, seq=seq)

    merged = _merge(attn, conv, w_ao_bf16, w_co_bf16, b_conv_o[0], gates, tm=2 * tm, tn=MXU_COLS)
    h1 = _residual_matmul(merged, w_out_bf16, h0, tm=2 * tm, tn=MXU_COLS, vmem_mib=52)

    act = _ffn_up(h1, ffn_norm_g[0], w_gate_up_bf16, ffn=ffn, tm=2 * tm, tn=256, chunks=16)
    h2 = _residual_matmul(act, w_down_bf16, h1, tm=tm, tn=512, vmem_mib=58, single_buffer_x=True)

    y = _rmsnorm(h2, final_norm_g, x.dtype, tm=512)
    return y.reshape(batch, seq, d)
```

```python
import functools

import jax
import jax.numpy as jnp
from jax import lax
from jax.experimental import pallas as pl
from jax.experimental.pallas import tpu as pltpu

N_META = 16
HEAD_DIM = 64
N_Q_HEADS = 32
N_KV_HEADS = 4
GROUP = N_Q_HEADS // N_KV_HEADS
WINDOW = 128
Q_DIM = N_Q_HEADS * HEAD_DIM
KV_DIM = N_KV_HEADS * HEAD_DIM
CONV_WIDTH = 31
ROPE_THETA = 10000.0
EPS = 1e-6

V7X_LANES = 128
V7X_BF16_ROWS = 16
V7X_VMEM_BYTES = 64 * 1024 * 1024
MXU_COLS = 256
DOT_ROWS = 1024
MIB = 1024 * 1024

F32 = jnp.float32
BF16 = jnp.bfloat16


def _params(vmem_mib, *sem):
    assert vmem_mib * MIB < V7X_VMEM_BYTES
    return pltpu.CompilerParams(dimension_semantics=sem, vmem_limit_bytes=vmem_mib * MIB)


def _after(x, dep):
    if dep is None:
        return x
    zero = lax.shift_right_logical(
        lax.shift_right_logical(lax.bitcast_convert_type(dep, jnp.uint32), jnp.uint32(16)), jnp.uint32(16))
    return lax.bitcast_convert_type(lax.bitcast_convert_type(x, jnp.uint32) | zero, x.dtype)


def _rmsnorm_kernel(x_ref, g_ref, o_ref):
    x = x_ref[...]
    ms = jnp.mean(x * x, axis=-1, keepdims=True)
    o_ref[...] = (x * lax.rsqrt(ms + EPS) * g_ref[...]).astype(o_ref.dtype)


def _rmsnorm(x, g, out_dtype, tm):
    rows, d = x.shape
    vmem_mib = max(16, 6 * tm * d * 4 // MIB)
    return pl.pallas_call(
        _rmsnorm_kernel,
        out_shape=jax.ShapeDtypeStruct((rows, d), out_dtype),
        grid=(rows // tm,),
        in_specs=[pl.BlockSpec((tm, d), lambda i: (i, 0)),
                  pl.BlockSpec((1, d), lambda i: (0, 0))],
        out_specs=pl.BlockSpec((tm, d), lambda i: (i, 0)),
        compiler_params=_params(vmem_mib, "arbitrary"),
        name="rmsnorm",
    )(x, g.reshape(1, d))


def _rmsnorm_chunk(x_ref, g_ref, store):
    dep = None
    for r0 in range(0, x_ref.shape[0], V7X_BF16_ROWS):
        x = x_ref[r0:r0 + V7X_BF16_ROWS, :]
        x = jnp.concatenate([_after(x[:, :V7X_LANES], dep), x[:, V7X_LANES:]], axis=1)
        ms = jnp.mean(x * x, axis=-1, keepdims=True)
        y = x * lax.rsqrt(ms + EPS) * g_ref[...]
        store(r0, y.astype(BF16))
        dep = y[:, :V7X_LANES]


def _next_tile_norm_maps(n_tiles, chunks):
    last = n_tiles * chunks - 1

    def chunk_map(i, j):
        return (jnp.where(i >= n_tiles, last, i * chunks + jnp.minimum(j, chunks - 1)), 0)

    def w_col(i, j):
        return jnp.where(i == 0, 0, j)

    def out_row(i):
        return jnp.maximum(i - 1, 0)

    return chunk_map, w_col, out_row


def _next_tile_norm_body(i, j, n_tiles, chunks, norm, matmul):
    do_norm = jnp.logical_and(j < chunks, i < n_tiles)

    @pl.when(jnp.logical_and(i == 0, do_norm))
    def _():
        norm(0)

    for slot in (0, 1):
        mine = jnp.logical_and(i > 0, i % 2 == slot)

        @pl.when(jnp.logical_and(mine, do_norm))
        def _():
            norm(slot)
            matmul(1 - slot)

        @pl.when(jnp.logical_and(mine, jnp.logical_not(do_norm)))
        def _():
            matmul(1 - slot)


def _rope(z, cos, sin_signed):
    lane = lax.broadcasted_iota(jnp.int32, (z.shape[0], V7X_LANES), 1)
    first_half = (lane & (HEAD_DIM - 1)) < HEAD_DIM // 2
    out = []
    for t in range(z.shape[1] // V7X_LANES):
        zt = z[:, t * V7X_LANES:(t + 1) * V7X_LANES]
        partner = jnp.where(first_half, pltpu.roll(zt, V7X_LANES - HEAD_DIM // 2, 1),
                            pltpu.roll(zt, HEAD_DIM // 2, 1))
        out.append(zt * cos + partner * sin_signed)
    return jnp.concatenate(out, axis=1)


def _qkv_tile(u, w_ref, b_ref, cos, sin_signed, o_ref, j, tn):
    assert tn == 2 * KV_DIM and KV_DIM == MXU_COLS
    is_q = j < Q_DIM // tn
    scale = jnp.where(is_q, HEAD_DIM ** -0.5, 1.0).astype(F32)
    for c0 in range(0, tn, MXU_COLS):
        cs = slice(c0, c0 + MXU_COLS)
        z = jnp.dot(u, w_ref[:, cs].astype(BF16), preferred_element_type=F32) + b_ref[:, cs]
        roped = _rope(z, cos, sin_signed)
        out = roped if c0 == 0 else jnp.where(is_q, roped, z)
        o_ref[:, cs] = (out * scale).astype(BF16)


def _qkv_kernel(u_ref, w_ref, b_ref, cos_ref, sin_ref, o_ref, *, tn, j0):
    _qkv_tile(u_ref[...], w_ref, b_ref, cos_ref[...], sin_ref[...], o_ref, pl.program_id(1) + j0, tn)


def _qkv_proj(u, w_in, b_in, cos, sin_signed, *, tm, tn, j0, nj):
    rows, d = u.shape
    assert rows == tm
    return pl.pallas_call(
        functools.partial(_qkv_kernel, tn=tn, j0=j0),
        out_shape=jax.ShapeDtypeStruct((rows, nj * tn), BF16),
        grid=(1, nj),
        in_specs=[pl.BlockSpec((tm, d), lambda i, j: (0, 0)),
                  pl.BlockSpec((d, tn), lambda i, j: (0, j + j0)),
                  pl.BlockSpec((1, tn), lambda i, j: (0, j + j0)),
                  pl.BlockSpec((tm, V7X_LANES), lambda i, j: (0, 0)),
                  pl.BlockSpec((tm, V7X_LANES), lambda i, j: (0, 0))],
        out_specs=pl.BlockSpec((tm, tn), lambda i, j: (0, j)),
        compiler_params=_params(32, "arbitrary", "arbitrary"),
        name="qkv_proj_meta",
    )(u, w_in, b_in, cos, sin_signed)


def _qkv_norm_kernel(x_ref, g_ref, w_ref, b_ref, cos_ref, sin_ref, o_ref, uo_ref, u0_ref, u1_ref,
                     *, tn, n_tiles, chunks):
    i, j = pl.program_id(0), pl.program_id(1)
    chunk = x_ref.shape[0]
    slots = (u0_ref, u1_ref)

    def norm(slot):
        base = pl.multiple_of(jnp.minimum(j, chunks - 1) * chunk, chunk)

        def store(r0, rows):
            slots[slot][pl.ds(pl.multiple_of(base + r0, V7X_BF16_ROWS), V7X_BF16_ROWS), :] = rows
            uo_ref[r0:r0 + V7X_BF16_ROWS, :] = rows

        _rmsnorm_chunk(x_ref, g_ref, store)

    def matmul(slot):
        _qkv_tile(slots[slot][...], w_ref, b_ref, cos_ref[...], sin_ref[...], o_ref, j, tn)

    _next_tile_norm_body(i, j, n_tiles, chunks, norm, matmul)


def _qkv_norm_proj(x, g, w_in, b_in, cos, sin_signed, *, tm, tn, nj, chunks, rows_per_seq):
    rows, d = x.shape
    n_tiles, seq_tiles, chunk = rows // tm, rows_per_seq // tm, tm // chunks
    chunk_map, w_col, out_row = _next_tile_norm_maps(n_tiles, chunks)
    return pl.pallas_call(
        functools.partial(_qkv_norm_kernel, tn=tn, n_tiles=n_tiles, chunks=chunks),
        out_shape=(jax.ShapeDtypeStruct((rows, nj * tn), BF16), jax.ShapeDtypeStruct((rows, d), BF16)),
        grid=(n_tiles + 1, nj),
        in_specs=[pl.BlockSpec((chunk, d), chunk_map),
                  pl.BlockSpec((1, d), lambda i, j: (0, 0)),
                  pl.BlockSpec((d, tn), lambda i, j: (0, w_col(i, j))),
                  pl.BlockSpec((1, tn), lambda i, j: (0, w_col(i, j))),
                  pl.BlockSpec((tm, V7X_LANES), lambda i, j: (out_row(i) % seq_tiles, 0)),
                  pl.BlockSpec((tm, V7X_LANES), lambda i, j: (out_row(i) % seq_tiles, 0))],
        out_specs=(pl.BlockSpec((tm, tn), lambda i, j: (out_row(i), w_col(i, j))),
                   pl.BlockSpec((chunk, d), chunk_map)),
        scratch_shapes=[pltpu.VMEM((tm, d), BF16), pltpu.VMEM((tm, d), BF16)],
        compiler_params=_params(58, "arbitrary", "arbitrary"),
        name="qkv_proj",
    )(x, g.reshape(1, d), w_in, b_in, cos, sin_signed)


def _sigmoid(x):
    return 0.5 * jnp.tanh(0.5 * x) + 0.5


def _glu(a, g):
    return a * _sigmoid(g)


def _swiglu(g, up):
    return g * _sigmoid(g) * up


def _side_cast_specs(weight, block_rows, nj, first_step=0):
    w_rows, w_cols = weight.shape
    assert w_rows % block_rows == 0 and block_rows % V7X_BF16_ROWS == 0
    last = w_rows // block_rows - 1
    spec = pl.BlockSpec((block_rows, w_cols),
                        lambda i, j: (jnp.clip(i * nj + j - first_step, 0, last), 0))
    return spec, jax.ShapeDtypeStruct((w_rows, w_cols), BF16), first_step + last + 1


def _pair_kernel(u_ref, wa_ref, wb_ref, ba_ref, bb_ref, *rest, combine):
    if len(rest) == 3:
        cast_in_ref, o_ref, cast_out_ref = rest
        cast_out_ref[...] = cast_in_ref[...].astype(BF16)
    else:
        (o_ref,) = rest
    a = jnp.dot(u_ref[...], wa_ref[...].astype(BF16), preferred_element_type=F32) + ba_ref[...]
    b = jnp.dot(u_ref[...], wb_ref[...].astype(BF16), preferred_element_type=F32) + bb_ref[...]
    o_ref[...] = combine(a, b).astype(o_ref.dtype)


def _pair_proj(u, w, b, *, col_a, col_b, width, tm, tn, combine, out_dtype, name, vmem_mib, cast=None):
    rows, d = u.shape
    ja, jb = col_a // tn, col_b // tn
    nj = width // tn
    assert ja * tn == col_a and jb * tn == col_b and nj * tn == width
    in_specs = [pl.BlockSpec((tm, d), lambda i, j: (i, 0)),
                pl.BlockSpec((d, tn), lambda i, j: (0, j + ja)),
                pl.BlockSpec((d, tn), lambda i, j: (0, j + jb)),
                pl.BlockSpec((1, tn), lambda i, j: (0, j + ja)),
                pl.BlockSpec((1, tn), lambda i, j: (0, j + jb))]
    args = [u, w, w, b, b]
    out_shape = jax.ShapeDtypeStruct((rows, width), out_dtype)
    out_specs = pl.BlockSpec((tm, tn), lambda i, j: (i, j))
    if cast is not None:
        spec, shape, steps = _side_cast_specs(cast[0], cast[1], nj)
        assert steps <= (rows // tm) * nj
        in_specs.append(spec)
        args.append(cast[0])
        out_shape, out_specs = (out_shape, shape), (out_specs, spec)
    return pl.pallas_call(
        functools.partial(_pair_kernel, combine=combine),
        out_shape=out_shape,
        grid=(rows // tm, nj),
        in_specs=in_specs,
        out_specs=out_specs,
        compiler_params=_params(vmem_mib, "arbitrary", "arbitrary"),
        name=name,
    )(*args)


CONV_HALO = 32
CONV_ROWS = 8
CONV_CHAINS = 2


def _conv_chunk(first, c_ref, cm_ref, w_ref, cb_ref, g_ref, b_ref, o_ref, ext_ref, y_ref):
    lane_tiles, rows = ext_ref.shape[0], c_ref.shape[0]
    lead = CONV_HALO - (CONV_WIDTH - 1)
    groups = rows // CONV_ROWS

    @pl.when(first)
    def _():
        for ct in range(lane_tiles):
            ext_ref[ct, 0:CONV_HALO - N_META, :] = jnp.zeros((CONV_HALO - N_META, V7X_LANES), F32)
            ext_ref[ct, CONV_HALO - N_META:CONV_HALO, :] = cm_ref[:, ct * V7X_LANES:(ct + 1) * V7X_LANES]

    @pl.when(jnp.logical_not(first))
    def _():
        for ct in range(lane_tiles):
            ext_ref[ct, 0:CONV_HALO, :] = ext_ref[ct, rows:rows + CONV_HALO, :]

    for ct in range(lane_tiles):
        ext_ref[ct, CONV_HALO:, :] = c_ref[:, ct * V7X_LANES:(ct + 1) * V7X_LANES]

    prev = [None] * CONV_CHAINS
    row_sum = [None] * groups
    n = 0
    for ct in range(lane_tiles):
        ls = slice(ct * V7X_LANES, (ct + 1) * V7X_LANES)
        for j in range(groups):
            acc = _after(jnp.broadcast_to(cb_ref[:, ls], (CONV_ROWS, V7X_LANES)), prev[n % CONV_CHAINS])
            for k in range(CONV_WIDTH):
                tap = jnp.broadcast_to(w_ref[k:k + 1, ls], (CONV_ROWS, V7X_LANES))
                r0 = j * CONV_ROWS + lead + k
                acc = acc + ext_ref[ct, r0:r0 + CONV_ROWS, :] * tap
            y_ref[j * CONV_ROWS:(j + 1) * CONV_ROWS, ls] = acc
            row_sum[j] = acc if row_sum[j] is None else row_sum[j] + acc
            prev[n % CONV_CHAINS] = acc
            n += 1

    inv_n = 1.0 / (lane_tiles * V7X_LANES)
    done = None
    for j in range(groups):
        rs = slice(j * CONV_ROWS, (j + 1) * CONV_ROWS)
        mu = jnp.sum(_after(row_sum[j], done), axis=1, keepdims=True) * inv_n
        parts = [None, None]
        for ct in range(lane_tiles):
            d = y_ref[rs, ct * V7X_LANES:(ct + 1) * V7X_LANES] - mu
            parts[ct % 2] = d * d if parts[ct % 2] is None else parts[ct % 2] + d * d
        var = jnp.sum(parts[0] + parts[1], axis=1, keepdims=True) * inv_n
        rstd = lax.rsqrt(var + EPS)
        for ct in range(lane_tiles):
            ls = slice(ct * V7X_LANES, (ct + 1) * V7X_LANES)
            yn = (y_ref[rs, ls] - mu) * rstd * g_ref[:, ls] + b_ref[:, ls]
            done = yn * _sigmoid(yn)
            y_ref[rs, ls] = done
    o_ref[...] = y_ref[...].astype(o_ref.dtype)


def _gates_conv_kernel(u_ref, w_ref, b_ref, c_ref, cm_ref, cw_ref, cb_ref, g_ref, bb_ref, cast_in_ref,
                       o_ref, co_ref, cast_out_ref, ext_ref, y_ref, *, seq_tiles):
    i, j = pl.program_id(0), pl.program_id(1)
    cast_out_ref[...] = cast_in_ref[...].astype(BF16)
    first = jnp.logical_and(i % seq_tiles == 0, j == 0)
    _conv_chunk(first, c_ref, cm_ref, cw_ref, cb_ref, g_ref, bb_ref, co_ref, ext_ref, y_ref)
    for r0 in range(0, o_ref.shape[0], DOT_ROWS):
        rs = slice(r0, r0 + DOT_ROWS)
        for c0 in range(0, o_ref.shape[1], MXU_COLS):
            cs = slice(c0, c0 + MXU_COLS)
            z = jnp.dot(u_ref[rs, :], w_ref[:, cs].astype(BF16), preferred_element_type=F32) + b_ref[:, cs]
            o_ref[rs, cs] = _sigmoid(z).astype(o_ref.dtype)


def _gates_conv(u, w, b, c, c_meta, conv_w, conv_b, ln_g, ln_b, cast, *, col0, width, tm, tn, rows_per_seq):
    rows, d = u.shape
    channels = c.shape[1]
    j0, nj = col0 // tn, width // tn
    assert j0 * tn == col0 and nj * tn == width and tm % nj == 0
    chunk = tm // nj
    assert chunk % V7X_BF16_ROWS == 0 and chunk >= CONV_HALO
    small = lambda i, j: (0, 0)
    cast_spec, cast_shape, cast_steps = _side_cast_specs(cast[0], cast[1], nj)
    assert cast_steps <= (rows // tm) * nj
    return pl.pallas_call(
        functools.partial(_gates_conv_kernel, seq_tiles=rows_per_seq // tm),
        out_shape=(jax.ShapeDtypeStruct((rows, width), BF16), jax.ShapeDtypeStruct((rows, channels), BF16),
                   cast_shape),
        grid=(rows // tm, nj),
        in_specs=[pl.BlockSpec((tm, d), lambda i, j: (i, 0)),
                  pl.BlockSpec((d, tn), lambda i, j: (0, j + j0)),
                  pl.BlockSpec((1, tn), lambda i, j: (0, j + j0)),
                  pl.BlockSpec((chunk, channels), lambda i, j: (i * nj + j, 0)),
                  pl.BlockSpec((N_META, channels), small),
                  pl.BlockSpec((CONV_WIDTH, channels), small),
                  pl.BlockSpec((1, channels), small),
                  pl.BlockSpec((1, channels), small),
                  pl.BlockSpec((1, channels), small),
                  cast_spec],
        out_specs=(pl.BlockSpec((tm, tn), lambda i, j: (i, j)),
                   pl.BlockSpec((chunk, channels), lambda i, j: (i * nj + j, 0)),
                   cast_spec),
        scratch_shapes=[pltpu.VMEM((channels // V7X_LANES, chunk + CONV_HALO, V7X_LANES), F32),
                        pltpu.VMEM((chunk, channels), F32)],
        compiler_params=_params(52, "arbitrary", "arbitrary"),
        name="gates_conv",
    )(u, w, b, c, c_meta, conv_w, conv_b.reshape(1, channels), ln_g.reshape(1, channels),
      ln_b.reshape(1, channels), cast[0])


ATT_KEYS = 3 * WINDOW
PAIRS = GROUP // 2


def _attn_kernel(sink_ref, q_ref, kc_ref, kp_ref, vc_ref, vp_ref, kvm_ref, *rest, cast_steps):
    n_cast = len(cast_steps)
    o_ref = rest[n_cast]
    step = pl.program_id(0) * pl.num_programs(1) + pl.program_id(1)
    for cast_in_ref, cast_out_ref, (lo, hi) in zip(rest[:n_cast], rest[n_cast + 1:], cast_steps):
        @pl.when(jnp.logical_and(step >= lo, step < hi))
        def _():
            cast_out_ref[...] = cast_in_ref[...].astype(BF16)
    first_block = pl.program_id(1) == 0
    rows = PAIRS * WINDOW
    r = lax.broadcasted_iota(jnp.int32, (rows, ATT_KEYS), 0) & (WINDOW - 1)
    c = lax.broadcasted_iota(jnp.int32, (rows, ATT_KEYS), 1)
    c_min = jnp.where(first_block, WINDOW, 0)
    band = (c > r) & (c <= r + WINDOW) & (c >= c_min)
    meta = (c >= 2 * WINDOW) & (c < 2 * WINDOW + N_META)
    bias = jnp.where(band | meta, 0.0, -jnp.inf).astype(F32)

    row_pair = lax.shift_right_logical(lax.broadcasted_iota(jnp.int32, (rows, 1), 0), WINDOW.bit_length() - 1)
    lane = lax.broadcasted_iota(jnp.int32, (ATT_KEYS, V7X_LANES), 1)
    low = lane < HEAD_DIM
    zero_pad = jnp.zeros((WINDOW - N_META, V7X_LANES), F32)

    units = []
    for g in range(N_KV_HEADS):
        tile, upper = g // 2, g % 2
        sl = slice(tile * V7X_LANES, (tile + 1) * V7X_LANES)

        def halves(prev_ref, cur_ref, meta_cols):
            x = jnp.concatenate([prev_ref[:, sl].astype(F32), cur_ref[:, sl].astype(F32),
                                 kvm_ref[:, meta_cols].astype(F32), zero_pad], axis=0)
            swapped = pltpu.roll(x, HEAD_DIM, 1)
            lo_src, hi_src = (swapped, x) if upper else (x, swapped)
            return (jnp.where(low, lo_src, 0.0).astype(BF16), jnp.where(low, 0.0, hi_src).astype(BF16))

        k_even, k_odd = halves(kp_ref, kc_ref, slice(tile * V7X_LANES, (tile + 1) * V7X_LANES))
        v_even, v_odd = halves(vp_ref, vc_ref, slice(KV_DIM + tile * V7X_LANES, KV_DIM + (tile + 1) * V7X_LANES))

        qp = jnp.concatenate(
            [q_ref[:, g * GROUP * HEAD_DIM + m * V7X_LANES: g * GROUP * HEAD_DIM + (m + 1) * V7X_LANES]
             for m in range(PAIRS)], axis=0)

        for parity, (kz, vz) in enumerate(((k_even, v_even), (k_odd, v_odd))):
            sink = jnp.zeros((rows, 1), F32)
            for m in range(PAIRS):
                sink = jnp.where(row_pair == m, sink_ref[g * GROUP + 2 * m + parity], sink)
            units.append((qp, kz, vz, sink))

    scores = [lax.dot_general(qp, kz, (((1,), (1,)), ((), ())), preferred_element_type=F32) + bias
              for qp, kz, _, _ in units]
    maxes = [jnp.maximum(jnp.max(s, axis=1, keepdims=True), unit[3]) for s, unit in zip(scores, units)]
    probs = [jnp.exp(s - mx) for s, mx in zip(scores, maxes)]
    denoms = [jnp.sum(p, axis=1, keepdims=True) + jnp.exp(unit[3] - mx)
              for p, mx, unit in zip(probs, maxes, units)]
    outs = [jnp.dot(p.astype(BF16), unit[2], preferred_element_type=F32) * (1.0 / den)
            for p, den, unit in zip(probs, denoms, units)]
    for g in range(N_KV_HEADS):
        out = outs[2 * g] + outs[2 * g + 1]
        for m in range(PAIRS):
            col = g * GROUP * HEAD_DIM + m * V7X_LANES
            o_ref[:, col:col + V7X_LANES] = out[m * WINDOW:(m + 1) * WINDOW, :].astype(o_ref.dtype)


def _attention(qkv, kv_meta, sinks, casts, *, batch, seq):
    nb = seq // WINDOW
    kcol, vcol = Q_DIM // KV_DIM, Q_DIM // KV_DIM + 1
    cast_specs, cast_shapes, cast_steps, step = [], [], [], 0
    for weight, block_rows in casts:
        spec, shape, end = _side_cast_specs(weight, block_rows, nb, first_step=step)
        cast_specs.append(spec)
        cast_shapes.append(shape)
        cast_steps.append((step, end))
        step = end
    assert step <= batch * nb

    def cur(col):
        return lambda b, j: (b * nb + j, col)

    def prev(col):
        return lambda b, j: (jnp.maximum(b * nb + j - 1, 0), col)

    return pl.pallas_call(
        functools.partial(_attn_kernel, cast_steps=tuple(cast_steps)),
        out_shape=(jax.ShapeDtypeStruct((batch * seq, Q_DIM), BF16), *cast_shapes),
        grid=(batch, nb),
        in_specs=[pl.BlockSpec(memory_space=pltpu.SMEM),
                  pl.BlockSpec((WINDOW, Q_DIM), lambda b, j: (b * nb + j, 0)),
                  pl.BlockSpec((WINDOW, KV_DIM), cur(kcol)),
                  pl.BlockSpec((WINDOW, KV_DIM), prev(kcol)),
                  pl.BlockSpec((WINDOW, KV_DIM), cur(vcol)),
                  pl.BlockSpec((WINDOW, KV_DIM), prev(vcol)),
                  pl.BlockSpec((N_META, 2 * KV_DIM), lambda b, j: (0, 0)),
                  *cast_specs],
        out_specs=(pl.BlockSpec((WINDOW, Q_DIM), lambda b, j: (b * nb + j, 0)), *cast_specs),
        compiler_params=_params(32, "arbitrary", "arbitrary"),
        name="swa_attention",
    )(sinks, qkv, qkv, qkv, qkv, qkv, kv_meta, *[weight for weight, _ in casts])


def _merge_kernel(a_ref, c_ref, wa_ref, wc_ref, bc_ref, ga_ref, gb_ref, o_ref):
    for r0 in range(0, o_ref.shape[0], DOT_ROWS):
        rs = slice(r0, r0 + DOT_ROWS)
        for c0 in range(0, o_ref.shape[1], MXU_COLS):
            cs = slice(c0, c0 + MXU_COLS)
            a = jnp.dot(a_ref[rs, :], wa_ref[:, cs].astype(BF16), preferred_element_type=F32)
            b = jnp.dot(c_ref[rs, :], wc_ref[:, cs].astype(BF16), preferred_element_type=F32) + bc_ref[:, cs]
            o_ref[rs, cs] = (ga_ref[rs, cs].astype(F32) * a + gb_ref[rs, cs].astype(F32) * b).astype(o_ref.dtype)


def _merge(attn, conv, w_ao, w_co, b_co, gates, *, tm, tn):
    assert tm % DOT_ROWS == 0 and tn % MXU_COLS == 0
    rows, k = attn.shape
    d = w_ao.shape[1]
    nj = d // tn
    return pl.pallas_call(
        _merge_kernel,
        out_shape=jax.ShapeDtypeStruct((rows, d), BF16),
        grid=(rows // tm, nj),
        in_specs=[pl.BlockSpec((tm, k), lambda i, j: (i, 0)),
                  pl.BlockSpec((tm, k), lambda i, j: (i, 0)),
                  pl.BlockSpec((k, tn), lambda i, j: (0, j)),
                  pl.BlockSpec((k, tn), lambda i, j: (0, j)),
                  pl.BlockSpec((1, tn), lambda i, j: (0, j)),
                  pl.BlockSpec((tm, tn), lambda i, j: (i, j)),
                  pl.BlockSpec((tm, tn), lambda i, j: (i, j + nj))],
        out_specs=pl.BlockSpec((tm, tn), lambda i, j: (i, j)),
        compiler_params=_params(58, "arbitrary", "arbitrary"),
        name="gated_merge",
    )(attn, conv, w_ao, w_co, b_co.reshape(1, d), gates, gates)


def _residual_kernel(x_ref, w_ref, r_ref, o_ref):
    for r0 in range(0, o_ref.shape[0], DOT_ROWS):
        rs = slice(r0, r0 + DOT_ROWS)
        for c0 in range(0, o_ref.shape[1], MXU_COLS):
            cs = slice(c0, c0 + MXU_COLS)
            o_ref[rs, cs] = r_ref[rs, cs] + jnp.dot(x_ref[rs, :], w_ref[:, cs].astype(BF16),
                                                    preferred_element_type=F32)


def _residual_matmul(x, w, resid, *, tm, tn, vmem_mib, single_buffer_x=False):
    rows, k = x.shape
    n = w.shape[1]
    x_spec = (pl.BlockSpec((tm, k), lambda i, j: (i, 0), pipeline_mode=pl.Buffered(1))
              if single_buffer_x else pl.BlockSpec((tm, k), lambda i, j: (i, 0)))
    return pl.pallas_call(
        _residual_kernel,
        out_shape=jax.ShapeDtypeStruct((rows, n), F32),
        grid=(rows // tm, n // tn),
        in_specs=[x_spec,
                  pl.BlockSpec((k, tn), lambda i, j: (0, j)),
                  pl.BlockSpec((tm, tn), lambda i, j: (i, j))],
        out_specs=pl.BlockSpec((tm, tn), lambda i, j: (i, j)),
        compiler_params=_params(vmem_mib, "arbitrary", "arbitrary"),
        name="residual_matmul",
    )(x, w, resid)


def _ffn_up_kernel(h_ref, g_ref, wa_ref, wb_ref, o_ref, u0_ref, u1_ref, *, n_tiles, chunks):
    i, j = pl.program_id(0), pl.program_id(1)
    chunk = h_ref.shape[0]
    slots = (u0_ref, u1_ref)

    def norm(slot):
        base = pl.multiple_of(jnp.minimum(j, chunks - 1) * chunk, chunk)

        def store(r0, rows):
            slots[slot][pl.ds(pl.multiple_of(base + r0, V7X_BF16_ROWS), V7X_BF16_ROWS), :] = rows

        _rmsnorm_chunk(h_ref, g_ref, store)

    def matmul(slot):
        for r0 in range(0, o_ref.shape[0], DOT_ROWS):
            u = slots[slot][r0:r0 + DOT_ROWS, :]
            a = jnp.dot(u, wa_ref[...].astype(BF16), preferred_element_type=F32)
            b = jnp.dot(u, wb_ref[...].astype(BF16), preferred_element_type=F32)
            o_ref[r0:r0 + DOT_ROWS, :] = _swiglu(a, b).astype(o_ref.dtype)

    _next_tile_norm_body(i, j, n_tiles, chunks, norm, matmul)


def _ffn_up(h, g, w, *, ffn, tm, tn, chunks):
    rows, d = h.shape
    n_tiles, nj, chunk = rows // tm, ffn // tn, tm // chunks
    assert nj * tn == ffn and nj >= chunks
    chunk_map, w_col, out_row = _next_tile_norm_maps(n_tiles, chunks)
    return pl.pallas_call(
        functools.partial(_ffn_up_kernel, n_tiles=n_tiles, chunks=chunks),
        out_shape=jax.ShapeDtypeStruct((rows, ffn), BF16),
        grid=(n_tiles + 1, nj),
        in_specs=[pl.BlockSpec((chunk, d), chunk_map),
                  pl.BlockSpec((1, d), lambda i, j: (0, 0)),
                  pl.BlockSpec((d, tn), lambda i, j: (0, w_col(i, j))),
                  pl.BlockSpec((d, tn), lambda i, j: (0, w_col(i, j) + nj))],
        out_specs=pl.BlockSpec((tm, tn), lambda i, j: (out_row(i), w_col(i, j))),
        scratch_shapes=[pltpu.VMEM((tm, d), BF16), pltpu.VMEM((tm, d), BF16)],
        compiler_params=_params(56, "arbitrary", "arbitrary"),
        name="ffn_gate_up",
    )(h, g.reshape(1, d), w, w)


def _rope_tables(first_pos, length):
    pos = jnp.arange(first_pos, first_pos + length, dtype=F32)
    inv_freq = ROPE_THETA ** (-jnp.arange(0, HEAD_DIM, 2, dtype=F32) / HEAD_DIM)
    ang = pos[:, None] * inv_freq[None, :]
    cos, sin = jnp.cos(ang), jnp.sin(ang)
    reps = V7X_LANES // HEAD_DIM
    return (jnp.tile(jnp.concatenate([cos, cos], axis=1), (1, reps)),
            jnp.tile(jnp.concatenate([-sin, sin], axis=1), (1, reps)))


def kernel(x, meta_tokens, mix_norm_g, w_in, b_in, attn_sinks, conv_w, conv_b, conv_ln_g, conv_ln_b,
           w_attn_o, w_conv_o, b_conv_o, w_out, ffn_norm_g, w_gate_up, w_down, final_norm_g):
    batch, seq, d = x.shape
    depth = w_in.shape[0]
    assert depth == 1, "one layer: the meta rows' outputs are never needed"
    conv_dim = conv_w.shape[-1]
    ffn = w_down.shape[1]
    in_dim = w_in.shape[-1]
    rows = batch * seq
    col_conv = Q_DIM + 2 * KV_DIM
    col_gate = col_conv + 2 * conv_dim
    assert in_dim == col_gate + 2 * d

    w_in2 = w_in.reshape(d, in_dim)
    b_in2 = b_in.reshape(1, in_dim)
    h0 = x.reshape(rows, d)
    tm = 1024
    qkv_tn = 512
    qkv_nj = (Q_DIM + 2 * KV_DIM) // qkv_tn

    cos, sin_signed = _rope_tables(N_META, seq)
    cos_m, sin_m = _rope_tables(0, N_META)

    qkv, u = _qkv_norm_proj(h0, mix_norm_g[0], w_in2, b_in2, cos, sin_signed, tm=tm, tn=qkv_tn,
                            nj=qkv_nj, chunks=qkv_nj - 1, rows_per_seq=seq)
    u_meta = _rmsnorm(meta_tokens, mix_norm_g[0], BF16, tm=N_META)
    kv_meta = _qkv_proj(u_meta, w_in2, b_in2, cos_m, sin_m, tm=N_META, tn=qkv_tn, j0=Q_DIM // qkv_tn, nj=1)
    glu = functools.partial(_pair_proj, w=w_in2, b=b_in2, col_a=col_conv, col_b=col_conv + conv_dim,
                            width=conv_dim, tn=256, combine=_glu, out_dtype=F32, name="conv_glu_proj")
    c, w_down_bf16 = glu(u, tm=tm, vmem_mib=52, cast=(w_down[0], 128))
    c_meta = glu(u_meta, tm=N_META, vmem_mib=32)

    gates, conv, w_gate_up_bf16 = _gates_conv(
        u, w_in2, b_in2, c, c_meta, conv_w.reshape(CONV_WIDTH, conv_dim), conv_b[0], conv_ln_g[0],
        conv_ln_b[0], (w_gate_up[0], V7X_BF16_ROWS), col0=col_gate, width=2 * d, tm=2 * tm, tn=MXU_COLS,
        rows_per_seq=seq)
    attn, w_ao_bf16, w_co_bf16, w_out_bf16 = _attention(
        qkv, kv_meta, attn_sinks[0], [(w_attn_o[0], 64), (w_conv_o[0], 64), (w_out[0], 64)],
        batch=batch, seq=seq)

    merged = _merge(attn, conv, w_ao_bf16, w_co_bf16, b_conv_o[0], gates, tm=2 * tm, tn=2 * MXU_COLS)
    h1 = _residual_matmul(merged, w_out_bf16, h0, tm=tm, tn=4 * MXU_COLS, vmem_mib=52)

    act = _ffn_up(h1, ffn_norm_g[0], w_gate_up_bf16, ffn=ffn, tm=2 * tm, tn=256, chunks=16)
    h2 = _residual_matmul(act, w_down_bf16, h1, tm=tm, tn=512, vmem_mib=58, single_buffer_x=True)

    y = _rmsnorm(h2, final_norm_g, x.dtype, tm=512)
    return y.reshape(batch, seq, d)
```

```python
import functools

import jax
import jax.numpy as jnp
from jax import lax
from jax.experimental import pallas as pl
from jax.experimental.pallas import tpu as pltpu

N_META = 16
HEAD_DIM = 64
N_Q_HEADS = 32
N_KV_HEADS = 4
GROUP = N_Q_HEADS // N_KV_HEADS
WINDOW = 128
Q_DIM = N_Q_HEADS * HEAD_DIM
KV_DIM = N_KV_HEADS * HEAD_DIM
CONV_WIDTH = 31
ROPE_THETA = 10000.0
EPS = 1e-6

V7X_LANES = 128
V7X_BF16_ROWS = 16
V7X_VMEM_BYTES = 64 * 1024 * 1024
MXU_COLS = 256
DOT_ROWS = 1024
MIB = 1024 * 1024

F32 = jnp.float32
BF16 = jnp.bfloat16


def _params(vmem_mib, *sem):
    assert vmem_mib * MIB < V7X_VMEM_BYTES
    return pltpu.CompilerParams(dimension_semantics=sem, vmem_limit_bytes=vmem_mib * MIB)


def _after(x, dep):
    if dep is None:
        return x
    zero = lax.shift_right_logical(
        lax.shift_right_logical(lax.bitcast_convert_type(dep, jnp.uint32), jnp.uint32(16)), jnp.uint32(16))
    return lax.bitcast_convert_type(lax.bitcast_convert_type(x, jnp.uint32) | zero, x.dtype)


def _rmsnorm_kernel(x_ref, g_ref, o_ref):
    x = x_ref[...]
    ms = jnp.mean(x * x, axis=-1, keepdims=True)
    o_ref[...] = (x * lax.rsqrt(ms + EPS) * g_ref[...]).astype(o_ref.dtype)


def _rmsnorm(x, g, out_dtype, tm):
    rows, d = x.shape
    vmem_mib = max(16, 6 * tm * d * 4 // MIB)
    return pl.pallas_call(
        _rmsnorm_kernel,
        out_shape=jax.ShapeDtypeStruct((rows, d), out_dtype),
        grid=(rows // tm,),
        in_specs=[pl.BlockSpec((tm, d), lambda i: (i, 0)),
                  pl.BlockSpec((1, d), lambda i: (0, 0))],
        out_specs=pl.BlockSpec((tm, d), lambda i: (i, 0)),
        compiler_params=_params(vmem_mib, "arbitrary"),
        name="rmsnorm",
    )(x, g.reshape(1, d))


def _rmsnorm_chunk(x_ref, g_ref, store):
    dep = None
    for r0 in range(0, x_ref.shape[0], V7X_BF16_ROWS):
        x = x_ref[r0:r0 + V7X_BF16_ROWS, :]
        x = jnp.concatenate([_after(x[:, :V7X_LANES], dep), x[:, V7X_LANES:]], axis=1)
        ms = jnp.mean(x * x, axis=-1, keepdims=True)
        y = x * lax.rsqrt(ms + EPS) * g_ref[...]
        store(r0, y.astype(BF16))
        dep = y[:, :V7X_LANES]


def _next_tile_norm_maps(n_tiles, chunks):
    last = n_tiles * chunks - 1

    def chunk_map(i, j):
        return (jnp.where(i >= n_tiles, last, i * chunks + jnp.minimum(j, chunks - 1)), 0)

    def w_col(i, j):
        return jnp.where(i == 0, 0, j)

    def out_row(i):
        return jnp.maximum(i - 1, 0)

    return chunk_map, w_col, out_row


def _next_tile_norm_body(i, j, n_tiles, chunks, norm, matmul):
    do_norm = jnp.logical_and(j < chunks, i < n_tiles)

    @pl.when(jnp.logical_and(i == 0, do_norm))
    def _():
        norm(0)

    for slot in (0, 1):
        mine = jnp.logical_and(i > 0, i % 2 == slot)

        @pl.when(jnp.logical_and(mine, do_norm))
        def _():
            norm(slot)
            matmul(1 - slot)

        @pl.when(jnp.logical_and(mine, jnp.logical_not(do_norm)))
        def _():
            matmul(1 - slot)


def _rope(z, cos, sin_signed):
    lane = lax.broadcasted_iota(jnp.int32, (z.shape[0], V7X_LANES), 1)
    first_half = (lane & (HEAD_DIM - 1)) < HEAD_DIM // 2
    out = []
    for t in range(z.shape[1] // V7X_LANES):
        zt = z[:, t * V7X_LANES:(t + 1) * V7X_LANES]
        partner = jnp.where(first_half, pltpu.roll(zt, V7X_LANES - HEAD_DIM // 2, 1),
                            pltpu.roll(zt, HEAD_DIM // 2, 1))
        out.append(zt * cos + partner * sin_signed)
    return jnp.concatenate(out, axis=1)


def _qkv_tile(u, w_ref, b_ref, cos, sin_signed, o_ref, j, tn):
    assert tn == 2 * KV_DIM and KV_DIM == MXU_COLS
    is_q = j < Q_DIM // tn
    scale = jnp.where(is_q, HEAD_DIM ** -0.5, 1.0).astype(F32)
    for c0 in range(0, tn, MXU_COLS):
        cs = slice(c0, c0 + MXU_COLS)
        z = jnp.dot(u, w_ref[:, cs].astype(BF16), preferred_element_type=F32) + b_ref[:, cs]
        roped = _rope(z, cos, sin_signed)
        out = roped if c0 == 0 else jnp.where(is_q, roped, z)
        o_ref[:, cs] = (out * scale).astype(BF16)


def _qkv_kernel(u_ref, w_ref, b_ref, cos_ref, sin_ref, o_ref, *, tn, j0):
    _qkv_tile(u_ref[...], w_ref, b_ref, cos_ref[...], sin_ref[...], o_ref, pl.program_id(1) + j0, tn)


def _qkv_proj(u, w_in, b_in, cos, sin_signed, *, tm, tn, j0, nj):
    rows, d = u.shape
    assert rows == tm
    return pl.pallas_call(
        functools.partial(_qkv_kernel, tn=tn, j0=j0),
        out_shape=jax.ShapeDtypeStruct((rows, nj * tn), BF16),
        grid=(1, nj),
        in_specs=[pl.BlockSpec((tm, d), lambda i, j: (0, 0)),
                  pl.BlockSpec((d, tn), lambda i, j: (0, j + j0)),
                  pl.BlockSpec((1, tn), lambda i, j: (0, j + j0)),
                  pl.BlockSpec((tm, V7X_LANES), lambda i, j: (0, 0)),
                  pl.BlockSpec((tm, V7X_LANES), lambda i, j: (0, 0))],
        out_specs=pl.BlockSpec((tm, tn), lambda i, j: (0, j)),
        compiler_params=_params(32, "arbitrary", "arbitrary"),
        name="qkv_proj_meta",
    )(u, w_in, b_in, cos, sin_signed)


def _qkv_norm_kernel(x_ref, g_ref, w_ref, b_ref, cos_ref, sin_ref, o_ref, uo_ref, u0_ref, u1_ref,
                     *, tn, n_tiles, chunks):
    i, j = pl.program_id(0), pl.program_id(1)
    chunk = x_ref.shape[0]
    slots = (u0_ref, u1_ref)

    def norm(slot):
        base = pl.multiple_of(jnp.minimum(j, chunks - 1) * chunk, chunk)

        def store(r0, rows):
            slots[slot][pl.ds(pl.multiple_of(base + r0, V7X_BF16_ROWS), V7X_BF16_ROWS), :] = rows
            uo_ref[r0:r0 + V7X_BF16_ROWS, :] = rows

        _rmsnorm_chunk(x_ref, g_ref, store)

    def matmul(slot):
        _qkv_tile(slots[slot][...], w_ref, b_ref, cos_ref[...], sin_ref[...], o_ref, j, tn)

    _next_tile_norm_body(i, j, n_tiles, chunks, norm, matmul)


def _qkv_norm_proj(x, g, w_in, b_in, cos, sin_signed, *, tm, tn, nj, chunks, rows_per_seq):
    rows, d = x.shape
    n_tiles, seq_tiles, chunk = rows // tm, rows_per_seq // tm, tm // chunks
    chunk_map, w_col, out_row = _next_tile_norm_maps(n_tiles, chunks)
    return pl.pallas_call(
        functools.partial(_qkv_norm_kernel, tn=tn, n_tiles=n_tiles, chunks=chunks),
        out_shape=(jax.ShapeDtypeStruct((rows, nj * tn), BF16), jax.ShapeDtypeStruct((rows, d), BF16)),
        grid=(n_tiles + 1, nj),
        in_specs=[pl.BlockSpec((chunk, d), chunk_map),
                  pl.BlockSpec((1, d), lambda i, j: (0, 0)),
                  pl.BlockSpec((d, tn), lambda i, j: (0, w_col(i, j))),
                  pl.BlockSpec((1, tn), lambda i, j: (0, w_col(i, j))),
                  pl.BlockSpec((tm, V7X_LANES), lambda i, j: (out_row(i) % seq_tiles, 0)),
                  pl.BlockSpec((tm, V7X_LANES), lambda i, j: (out_row(i) % seq_tiles, 0))],
        out_specs=(pl.BlockSpec((tm, tn), lambda i, j: (out_row(i), w_col(i, j))),
                   pl.BlockSpec((chunk, d), chunk_map)),
        scratch_shapes=[pltpu.VMEM((tm, d), BF16), pltpu.VMEM((tm, d), BF16)],
        compiler_params=_params(58, "arbitrary", "arbitrary"),
        name="qkv_proj",
    )(x, g.reshape(1, d), w_in, b_in, cos, sin_signed)


def _sigmoid(x):
    return 0.5 * jnp.tanh(0.5 * x) + 0.5


def _glu(a, g):
    return a * _sigmoid(g)


def _swiglu(g, up):
    return g * _sigmoid(g) * up


def _side_cast_specs(weight, block_rows, nj, first_step=0):
    w_rows, w_cols = weight.shape
    assert w_rows % block_rows == 0 and block_rows % V7X_BF16_ROWS == 0
    last = w_rows // block_rows - 1
    spec = pl.BlockSpec((block_rows, w_cols),
                        lambda i, j: (jnp.clip(i * nj + j - first_step, 0, last), 0))
    return spec, jax.ShapeDtypeStruct((w_rows, w_cols), BF16), first_step + last + 1


def _pair_kernel(u_ref, wa_ref, wb_ref, ba_ref, bb_ref, *rest, combine):
    if len(rest) == 3:
        cast_in_ref, o_ref, cast_out_ref = rest
        cast_out_ref[...] = cast_in_ref[...].astype(BF16)
    else:
        (o_ref,) = rest
    a = jnp.dot(u_ref[...], wa_ref[...].astype(BF16), preferred_element_type=F32) + ba_ref[...]
    b = jnp.dot(u_ref[...], wb_ref[...].astype(BF16), preferred_element_type=F32) + bb_ref[...]
    o_ref[...] = combine(a, b).astype(o_ref.dtype)


def _pair_proj(u, w, b, *, col_a, col_b, width, tm, tn, combine, out_dtype, name, vmem_mib, cast=None):
    rows, d = u.shape
    ja, jb = col_a // tn, col_b // tn
    nj = width // tn
    assert ja * tn == col_a and jb * tn == col_b and nj * tn == width
    in_specs = [pl.BlockSpec((tm, d), lambda i, j: (i, 0)),
                pl.BlockSpec((d, tn), lambda i, j: (0, j + ja)),
                pl.BlockSpec((d, tn), lambda i, j: (0, j + jb)),
                pl.BlockSpec((1, tn), lambda i, j: (0, j + ja)),
                pl.BlockSpec((1, tn), lambda i, j: (0, j + jb))]
    args = [u, w, w, b, b]
    out_shape = jax.ShapeDtypeStruct((rows, width), out_dtype)
    out_specs = pl.BlockSpec((tm, tn), lambda i, j: (i, j))
    if cast is not None:
        spec, shape, steps = _side_cast_specs(cast[0], cast[1], nj)
        assert steps <= (rows // tm) * nj
        in_specs.append(spec)
        args.append(cast[0])
        out_shape, out_specs = (out_shape, shape), (out_specs, spec)
    return pl.pallas_call(
        functools.partial(_pair_kernel, combine=combine),
        out_shape=out_shape,
        grid=(rows // tm, nj),
        in_specs=in_specs,
        out_specs=out_specs,
        compiler_params=_params(vmem_mib, "arbitrary", "arbitrary"),
        name=name,
    )(*args)


CONV_HALO = 32
CONV_ROWS = 8
CONV_CHAINS = 2


def _conv_chunk(first, c_ref, cm_ref, w_ref, cb_ref, g_ref, b_ref, o_ref, ext_ref, y_ref):
    lane_tiles, rows = ext_ref.shape[0], c_ref.shape[0]
    lead = CONV_HALO - (CONV_WIDTH - 1)
    groups = rows // CONV_ROWS

    @pl.when(first)
    def _():
        for ct in range(lane_tiles):
            ext_ref[ct, 0:CONV_HALO - N_META, :] = jnp.zeros((CONV_HALO - N_META, V7X_LANES), F32)
            ext_ref[ct, CONV_HALO - N_META:CONV_HALO, :] = cm_ref[:, ct * V7X_LANES:(ct + 1) * V7X_LANES]

    @pl.when(jnp.logical_not(first))
    def _():
        for ct in range(lane_tiles):
            ext_ref[ct, 0:CONV_HALO, :] = ext_ref[ct, rows:rows + CONV_HALO, :]

    for ct in range(lane_tiles):
        ext_ref[ct, CONV_HALO:, :] = c_ref[:, ct * V7X_LANES:(ct + 1) * V7X_LANES]

    prev = [None] * CONV_CHAINS
    row_sum = [None] * groups
    n = 0
    for ct in range(lane_tiles):
        ls = slice(ct * V7X_LANES, (ct + 1) * V7X_LANES)
        for j in range(groups):
            acc = _after(jnp.broadcast_to(cb_ref[:, ls], (CONV_ROWS, V7X_LANES)), prev[n % CONV_CHAINS])
            for k in range(CONV_WIDTH):
                tap = jnp.broadcast_to(w_ref[k:k + 1, ls], (CONV_ROWS, V7X_LANES))
                r0 = j * CONV_ROWS + lead + k
                acc = acc + ext_ref[ct, r0:r0 + CONV_ROWS, :] * tap
            y_ref[j * CONV_ROWS:(j + 1) * CONV_ROWS, ls] = acc
            row_sum[j] = acc if row_sum[j] is None else row_sum[j] + acc
            prev[n % CONV_CHAINS] = acc
            n += 1

    inv_n = 1.0 / (lane_tiles * V7X_LANES)
    done = None
    for j in range(groups):
        rs = slice(j * CONV_ROWS, (j + 1) * CONV_ROWS)
        mu = jnp.sum(_after(row_sum[j], done), axis=1, keepdims=True) * inv_n
        parts = [None, None]
        for ct in range(lane_tiles):
            d = y_ref[rs, ct * V7X_LANES:(ct + 1) * V7X_LANES] - mu
            parts[ct % 2] = d * d if parts[ct % 2] is None else parts[ct % 2] + d * d
        var = jnp.sum(parts[0] + parts[1], axis=1, keepdims=True) * inv_n
        rstd = lax.rsqrt(var + EPS)
        for ct in range(lane_tiles):
            ls = slice(ct * V7X_LANES, (ct + 1) * V7X_LANES)
            yn = (y_ref[rs, ls] - mu) * rstd * g_ref[:, ls] + b_ref[:, ls]
            done = yn * _sigmoid(yn)
            y_ref[rs, ls] = done
    o_ref[...] = y_ref[...].astype(o_ref.dtype)


def _gates_conv_kernel(u_ref, w_ref, b_ref, c_ref, cm_ref, cw_ref, cb_ref, g_ref, bb_ref, cast_in_ref,
                       o_ref, co_ref, cast_out_ref, ext_ref, y_ref, *, seq_tiles):
    i, j = pl.program_id(0), pl.program_id(1)
    cast_out_ref[...] = cast_in_ref[...].astype(BF16)
    first = jnp.logical_and(i % seq_tiles == 0, j == 0)
    _conv_chunk(first, c_ref, cm_ref, cw_ref, cb_ref, g_ref, bb_ref, co_ref, ext_ref, y_ref)
    for r0 in range(0, o_ref.shape[0], DOT_ROWS):
        rs = slice(r0, r0 + DOT_ROWS)
        for c0 in range(0, o_ref.shape[1], MXU_COLS):
            cs = slice(c0, c0 + MXU_COLS)
            z = jnp.dot(u_ref[rs, :], w_ref[:, cs].astype(BF16), preferred_element_type=F32) + b_ref[:, cs]
            o_ref[rs, cs] = _sigmoid(z).astype(o_ref.dtype)


def _gates_conv(u, w, b, c, c_meta, conv_w, conv_b, ln_g, ln_b, cast, *, col0, width, tm, tn, rows_per_seq):
    rows, d = u.shape
    channels = c.shape[1]
    j0, nj = col0 // tn, width // tn
    assert j0 * tn == col0 and nj * tn == width and tm % nj == 0
    chunk = tm // nj
    assert chunk % V7X_BF16_ROWS == 0 and chunk >= CONV_HALO
    small = lambda i, j: (0, 0)
    cast_spec, cast_shape, cast_steps = _side_cast_specs(cast[0], cast[1], nj)
    assert cast_steps <= (rows // tm) * nj
    return pl.pallas_call(
        functools.partial(_gates_conv_kernel, seq_tiles=rows_per_seq // tm),
        out_shape=(jax.ShapeDtypeStruct((rows, width), BF16), jax.ShapeDtypeStruct((rows, channels), BF16),
                   cast_shape),
        grid=(rows // tm, nj),
        in_specs=[pl.BlockSpec((tm, d), lambda i, j: (i, 0), pipeline_mode=pl.Buffered(1)),
                  pl.BlockSpec((d, tn), lambda i, j: (0, j + j0)),
                  pl.BlockSpec((1, tn), lambda i, j: (0, j + j0)),
                  pl.BlockSpec((chunk, channels), lambda i, j: (i * nj + j, 0)),
                  pl.BlockSpec((N_META, channels), small),
                  pl.BlockSpec((CONV_WIDTH, channels), small),
                  pl.BlockSpec((1, channels), small),
                  pl.BlockSpec((1, channels), small),
                  pl.BlockSpec((1, channels), small),
                  cast_spec],
        out_specs=(pl.BlockSpec((tm, tn), lambda i, j: (i, j)),
                   pl.BlockSpec((chunk, channels), lambda i, j: (i * nj + j, 0)),
                   cast_spec),
        scratch_shapes=[pltpu.VMEM((channels // V7X_LANES, chunk + CONV_HALO, V7X_LANES), F32),
                        pltpu.VMEM((chunk, channels), F32)],
        compiler_params=_params(58, "arbitrary", "arbitrary"),
        name="gates_conv",
    )(u, w, b, c, c_meta, conv_w, conv_b.reshape(1, channels), ln_g.reshape(1, channels),
      ln_b.reshape(1, channels), cast[0])


ATT_KEYS = 3 * WINDOW
PAIRS = GROUP // 2


def _attn_kernel(sink_ref, q_ref, kc_ref, kp_ref, vc_ref, vp_ref, kvm_ref, *rest, cast_steps):
    n_cast = len(cast_steps)
    o_ref = rest[n_cast]
    step = pl.program_id(0) * pl.num_programs(1) + pl.program_id(1)
    for cast_in_ref, cast_out_ref, (lo, hi) in zip(rest[:n_cast], rest[n_cast + 1:], cast_steps):
        @pl.when(jnp.logical_and(step >= lo, step < hi))
        def _():
            cast_out_ref[...] = cast_in_ref[...].astype(BF16)
    first_block = pl.program_id(1) == 0
    rows = PAIRS * WINDOW
    r = lax.broadcasted_iota(jnp.int32, (rows, ATT_KEYS), 0) & (WINDOW - 1)
    c = lax.broadcasted_iota(jnp.int32, (rows, ATT_KEYS), 1)
    c_min = jnp.where(first_block, WINDOW, 0)
    band = (c > r) & (c <= r + WINDOW) & (c >= c_min)
    meta = (c >= 2 * WINDOW) & (c < 2 * WINDOW + N_META)
    bias = jnp.where(band | meta, 0.0, -jnp.inf).astype(F32)

    row_pair = lax.shift_right_logical(lax.broadcasted_iota(jnp.int32, (rows, 1), 0), WINDOW.bit_length() - 1)
    lane = lax.broadcasted_iota(jnp.int32, (ATT_KEYS, V7X_LANES), 1)
    low = lane < HEAD_DIM
    zero_pad = jnp.zeros((WINDOW - N_META, V7X_LANES), F32)

    units = []
    for g in range(N_KV_HEADS):
        tile, upper = g // 2, g % 2
        sl = slice(tile * V7X_LANES, (tile + 1) * V7X_LANES)

        def halves(prev_ref, cur_ref, meta_cols):
            x = jnp.concatenate([prev_ref[:, sl].astype(F32), cur_ref[:, sl].astype(F32),
                                 kvm_ref[:, meta_cols].astype(F32), zero_pad], axis=0)
            swapped = pltpu.roll(x, HEAD_DIM, 1)
            lo_src, hi_src = (swapped, x) if upper else (x, swapped)
            return (jnp.where(low, lo_src, 0.0).astype(BF16), jnp.where(low, 0.0, hi_src).astype(BF16))

        k_even, k_odd = halves(kp_ref, kc_ref, slice(tile * V7X_LANES, (tile + 1) * V7X_LANES))
        v_even, v_odd = halves(vp_ref, vc_ref, slice(KV_DIM + tile * V7X_LANES, KV_DIM + (tile + 1) * V7X_LANES))

        qp = jnp.concatenate(
            [q_ref[:, g * GROUP * HEAD_DIM + m * V7X_LANES: g * GROUP * HEAD_DIM + (m + 1) * V7X_LANES]
             for m in range(PAIRS)], axis=0)

        for parity, (kz, vz) in enumerate(((k_even, v_even), (k_odd, v_odd))):
            sink = jnp.zeros((rows, 1), F32)
            for m in range(PAIRS):
                sink = jnp.where(row_pair == m, sink_ref[g * GROUP + 2 * m + parity], sink)
            units.append((qp, kz, vz, sink))

    scores = [lax.dot_general(qp, kz, (((1,), (1,)), ((), ())), preferred_element_type=F32) + bias
              for qp, kz, _, _ in units]
    maxes = [jnp.maximum(jnp.max(s, axis=1, keepdims=True), unit[3]) for s, unit in zip(scores, units)]
    probs = [jnp.exp(s - mx) for s, mx in zip(scores, maxes)]
    denoms = [jnp.sum(p, axis=1, keepdims=True) + jnp.exp(unit[3] - mx)
              for p, mx, unit in zip(probs, maxes, units)]
    outs = [jnp.dot(p.astype(BF16), unit[2], preferred_element_type=F32) * (1.0 / den)
            for p, den, unit in zip(probs, denoms, units)]
    for g in range(N_KV_HEADS):
        out = outs[2 * g] + outs[2 * g + 1]
        for m in range(PAIRS):
            col = g * GROUP * HEAD_DIM + m * V7X_LANES
            o_ref[:, col:col + V7X_LANES] = out[m * WINDOW:(m + 1) * WINDOW, :].astype(o_ref.dtype)


def _attention(qkv, kv_meta, sinks, casts, *, batch, seq):
    nb = seq // WINDOW
    kcol, vcol = Q_DIM // KV_DIM, Q_DIM // KV_DIM + 1
    cast_specs, cast_shapes, cast_steps, step = [], [], [], 0
    for weight, block_rows in casts:
        spec, shape, end = _side_cast_specs(weight, block_rows, nb, first_step=step)
        cast_specs.append(spec)
        cast_shapes.append(shape)
        cast_steps.append((step, end))
        step = end
    assert step <= batch * nb

    def cur(col):
        return lambda b, j: (b * nb + j, col)

    def prev(col):
        return lambda b, j: (jnp.maximum(b * nb + j - 1, 0), col)

    return pl.pallas_call(
        functools.partial(_attn_kernel, cast_steps=tuple(cast_steps)),
        out_shape=(jax.ShapeDtypeStruct((batch * seq, Q_DIM), BF16), *cast_shapes),
        grid=(batch, nb),
        in_specs=[pl.BlockSpec(memory_space=pltpu.SMEM),
                  pl.BlockSpec((WINDOW, Q_DIM), lambda b, j: (b * nb + j, 0)),
                  pl.BlockSpec((WINDOW, KV_DIM), cur(kcol)),
                  pl.BlockSpec((WINDOW, KV_DIM), prev(kcol)),
                  pl.BlockSpec((WINDOW, KV_DIM), cur(vcol)),
                  pl.BlockSpec((WINDOW, KV_DIM), prev(vcol)),
                  pl.BlockSpec((N_META, 2 * KV_DIM), lambda b, j: (0, 0)),
                  *cast_specs],
        out_specs=(pl.BlockSpec((WINDOW, Q_DIM), lambda b, j: (b * nb + j, 0)), *cast_specs),
        compiler_params=_params(32, "arbitrary", "arbitrary"),
        name="swa_attention",
    )(sinks, qkv, qkv, qkv, qkv, qkv, kv_meta, *[weight for weight, _ in casts])


def _merge_kernel(a_ref, c_ref, wa_ref, wc_ref, bc_ref, ga_ref, gb_ref, o_ref):
    for r0 in range(0, o_ref.shape[0], DOT_ROWS):
        rs = slice(r0, r0 + DOT_ROWS)
        for c0 in range(0, o_ref.shape[1], MXU_COLS):
            cs = slice(c0, c0 + MXU_COLS)
            a = jnp.dot(a_ref[rs, :], wa_ref[:, cs].astype(BF16), preferred_element_type=F32)
            b = jnp.dot(c_ref[rs, :], wc_ref[:, cs].astype(BF16), preferred_element_type=F32) + bc_ref[:, cs]
            o_ref[rs, cs] = (ga_ref[rs, cs].astype(F32) * a + gb_ref[rs, cs].astype(F32) * b).astype(o_ref.dtype)


def _merge(attn, conv, w_ao, w_co, b_co, gates, *, tm, tn):
    assert tm % DOT_ROWS == 0 and tn % MXU_COLS == 0
    rows, k = attn.shape
    d = w_ao.shape[1]
    nj = d // tn
    return pl.pallas_call(
        _merge_kernel,
        out_shape=jax.ShapeDtypeStruct((rows, d), BF16),
        grid=(rows // tm, nj),
        in_specs=[pl.BlockSpec((tm, k), lambda i, j: (i, 0)),
                  pl.BlockSpec((tm, k), lambda i, j: (i, 0)),
                  pl.BlockSpec((k, tn), lambda i, j: (0, j)),
                  pl.BlockSpec((k, tn), lambda i, j: (0, j)),
                  pl.BlockSpec((1, tn), lambda i, j: (0, j)),
                  pl.BlockSpec((tm, tn), lambda i, j: (i, j)),
                  pl.BlockSpec((tm, tn), lambda i, j: (i, j + nj))],
        out_specs=pl.BlockSpec((tm, tn), lambda i, j: (i, j)),
        compiler_params=_params(58, "arbitrary", "arbitrary"),
        name="gated_merge",
    )(attn, conv, w_ao, w_co, b_co.reshape(1, d), gates, gates)


def _residual_kernel(x_ref, w_ref, r_ref, o_ref):
    for r0 in range(0, o_ref.shape[0], DOT_ROWS):
        rs = slice(r0, r0 + DOT_ROWS)
        for c0 in range(0, o_ref.shape[1], MXU_COLS):
            cs = slice(c0, c0 + MXU_COLS)
            o_ref[rs, cs] = r_ref[rs, cs] + jnp.dot(x_ref[rs, :], w_ref[:, cs].astype(BF16),
                                                    preferred_element_type=F32)


def _residual_matmul(x, w, resid, *, tm, tn, vmem_mib, single_buffer_x=False):
    rows, k = x.shape
    n = w.shape[1]
    x_spec = (pl.BlockSpec((tm, k), lambda i, j: (i, 0), pipeline_mode=pl.Buffered(1))
              if single_buffer_x else pl.BlockSpec((tm, k), lambda i, j: (i, 0)))
    return pl.pallas_call(
        _residual_kernel,
        out_shape=jax.ShapeDtypeStruct((rows, n), F32),
        grid=(rows // tm, n // tn),
        in_specs=[x_spec,
                  pl.BlockSpec((k, tn), lambda i, j: (0, j)),
                  pl.BlockSpec((tm, tn), lambda i, j: (i, j))],
        out_specs=pl.BlockSpec((tm, tn), lambda i, j: (i, j)),
        compiler_params=_params(vmem_mib, "arbitrary", "arbitrary"),
        name="residual_matmul",
    )(x, w, resid)


def _ffn_up_kernel(h_ref, g_ref, wa_ref, wb_ref, o_ref, u0_ref, u1_ref, *, n_tiles, chunks):
    i, j = pl.program_id(0), pl.program_id(1)
    chunk = h_ref.shape[0]
    slots = (u0_ref, u1_ref)

    def norm(slot):
        base = pl.multiple_of(jnp.minimum(j, chunks - 1) * chunk, chunk)

        def store(r0, rows):
            slots[slot][pl.ds(pl.multiple_of(base + r0, V7X_BF16_ROWS), V7X_BF16_ROWS), :] = rows

        _rmsnorm_chunk(h_ref, g_ref, store)

    def matmul(slot):
        for r0 in range(0, o_ref.shape[0], DOT_ROWS):
            u = slots[slot][r0:r0 + DOT_ROWS, :]
            a = jnp.dot(u, wa_ref[...].astype(BF16), preferred_element_type=F32)
            b = jnp.dot(u, wb_ref[...].astype(BF16), preferred_element_type=F32)
            o_ref[r0:r0 + DOT_ROWS, :] = _swiglu(a, b).astype(o_ref.dtype)

    _next_tile_norm_body(i, j, n_tiles, chunks, norm, matmul)


def _ffn_up(h, g, w, *, ffn, tm, tn, chunks):
    rows, d = h.shape
    n_tiles, nj, chunk = rows // tm, ffn // tn, tm // chunks
    assert nj * tn == ffn and nj >= chunks
    chunk_map, w_col, out_row = _next_tile_norm_maps(n_tiles, chunks)
    return pl.pallas_call(
        functools.partial(_ffn_up_kernel, n_tiles=n_tiles, chunks=chunks),
        out_shape=jax.ShapeDtypeStruct((rows, ffn), BF16),
        grid=(n_tiles + 1, nj),
        in_specs=[pl.BlockSpec((chunk, d), chunk_map),
                  pl.BlockSpec((1, d), lambda i, j: (0, 0)),
                  pl.BlockSpec((d, tn), lambda i, j: (0, w_col(i, j))),
                  pl.BlockSpec((d, tn), lambda i, j: (0, w_col(i, j) + nj))],
        out_specs=pl.BlockSpec((tm, tn), lambda i, j: (out_row(i), w_col(i, j))),
        scratch_shapes=[pltpu.VMEM((tm, d), BF16), pltpu.VMEM((tm, d), BF16)],
        compiler_params=_params(56, "arbitrary", "arbitrary"),
        name="ffn_gate_up",
    )(h, g.reshape(1, d), w, w)


def _rope_tables(first_pos, length):
    pos = jnp.arange(first_pos, first_pos + length, dtype=F32)
    inv_freq = ROPE_THETA ** (-jnp.arange(0, HEAD_DIM, 2, dtype=F32) / HEAD_DIM)
    ang = pos[:, None] * inv_freq[None, :]
    cos, sin = jnp.cos(ang), jnp.sin(ang)
    reps = V7X_LANES // HEAD_DIM
    return (jnp.tile(jnp.concatenate([cos, cos], axis=1), (1, reps)),
            jnp.tile(jnp.concatenate([-sin, sin], axis=1), (1, reps)))


def kernel(x, meta_tokens, mix_norm_g, w_in, b_in, attn_sinks, conv_w, conv_b, conv_ln_g, conv_ln_b,
           w_attn_o, w_conv_o, b_conv_o, w_out, ffn_norm_g, w_gate_up, w_down, final_norm_g):
    batch, seq, d = x.shape
    depth = w_in.shape[0]
    assert depth == 1, "one layer: the meta rows' outputs are never needed"
    conv_dim = conv_w.shape[-1]
    ffn = w_down.shape[1]
    in_dim = w_in.shape[-1]
    rows = batch * seq
    col_conv = Q_DIM + 2 * KV_DIM
    col_gate = col_conv + 2 * conv_dim
    assert in_dim == col_gate + 2 * d

    w_in2 = w_in.reshape(d, in_dim)
    b_in2 = b_in.reshape(1, in_dim)
    h0 = x.reshape(rows, d)
    tm = 1024
    qkv_tn = 512
    qkv_nj = (Q_DIM + 2 * KV_DIM) // qkv_tn

    cos, sin_signed = _rope_tables(N_META, seq)
    cos_m, sin_m = _rope_tables(0, N_META)

    qkv, u = _qkv_norm_proj(h0, mix_norm_g[0], w_in2, b_in2, cos, sin_signed, tm=tm, tn=qkv_tn,
                            nj=qkv_nj, chunks=qkv_nj - 1, rows_per_seq=seq)
    u_meta = _rmsnorm(meta_tokens, mix_norm_g[0], BF16, tm=N_META)
    kv_meta = _qkv_proj(u_meta, w_in2, b_in2, cos_m, sin_m, tm=N_META, tn=qkv_tn, j0=Q_DIM // qkv_tn, nj=1)
    glu = functools.partial(_pair_proj, w=w_in2, b=b_in2, col_a=col_conv, col_b=col_conv + conv_dim,
                            width=conv_dim, tn=256, combine=_glu, out_dtype=F32, name="conv_glu_proj")
    c, w_down_bf16 = glu(u, tm=tm, vmem_mib=52, cast=(w_down[0], 128))
    c_meta = glu(u_meta, tm=N_META, vmem_mib=32)

    gates, conv, w_gate_up_bf16 = _gates_conv(
        u, w_in2, b_in2, c, c_meta, conv_w.reshape(CONV_WIDTH, conv_dim), conv_b[0], conv_ln_g[0],
        conv_ln_b[0], (w_gate_up[0], 2 * V7X_BF16_ROWS), col0=col_gate, width=2 * d, tm=2 * tm,
        tn=2 * MXU_COLS, rows_per_seq=seq)
    attn, w_ao_bf16, w_co_bf16, w_out_bf16 = _attention(
        qkv, kv_meta, attn_sinks[0], [(w_attn_o[0], 64), (w_conv_o[0], 64), (w_out[0], 64)],
        batch=batch, seq=seq)

    merged = _merge(attn, conv, w_ao_bf16, w_co_bf16, b_conv_o[0], gates, tm=2 * tm, tn=2 * MXU_COLS)
    h1 = _residual_matmul(merged, w_out_bf16, h0, tm=tm, tn=4 * MXU_COLS, vmem_mib=52)

    act = _ffn_up(h1, ffn_norm_g[0], w_gate_up_bf16, ffn=ffn, tm=2 * tm, tn=256, chunks=16)
    h2 = _residual_matmul(act, w_down_bf16, h1, tm=tm, tn=512, vmem_mib=58, single_buffer_x=True)

    y = _rmsnorm(h2, final_norm_g, x.dtype, tm=512)
    return y.reshape(batch, seq, d)
```
